```python
import jax, jax.numpy as jnp
from jax import lax
import numpy as np

D_MODEL = 1024
BATCH = 16
SEQ = 2048
DEPTH = 1

PLE_DIM = 256
SB_HEADS = 8
SB_HEAD_DIM = 64
SB_BLOCK = 128
RET_HEADS = 8
RET_QK_DIM = 64
RET_V_DIM = 128
RET_CHUNK = 128
ROPE_BASE = 10000.0
D_FF = 2816
CONV_WIDTH = 3
EPS = 1e-6

SB_W = SB_HEADS * SB_HEAD_DIM
RET_QK_W = RET_HEADS * RET_QK_DIM
RET_V_W = RET_HEADS * RET_V_DIM
IN_SPLITS = (SB_W, SB_W, SB_W, RET_QK_W, RET_QK_W, RET_V_W, RET_V_W, D_MODEL, D_MODEL)
D_IN = SB_W * 3 + RET_QK_W * 2 + RET_V_W * 2 + D_MODEL * 2

kernel_name = "hybrid_stickbreak_retention_convglu_ple"


def rmsnorm(x, gain):
    xf = x.astype(jnp.float32)
    xf = xf * lax.rsqrt(jnp.mean(xf * xf, axis=-1, keepdims=True) + EPS)
    return (xf * gain.astype(jnp.float32)).astype(x.dtype)


def head_rmsnorm(x, gain):
    xf = x.astype(jnp.float32)
    xf = xf * lax.rsqrt(jnp.mean(xf * xf, axis=-1, keepdims=True) + EPS)
    return (xf * gain.astype(jnp.float32)[None, :, None, :]).astype(x.dtype)


def split_cols(y):
    out = []
    off = 0
    for w in IN_SPLITS:
        out.append(y[..., off:off + w])
        off += w
    return out


def to_heads(t, n_heads):
    b, s, _ = t.shape
    return t.reshape(b, s, n_heads, -1).transpose(0, 2, 1, 3)


def from_heads(t):
    b, h, s, d = t.shape
    return t.transpose(0, 2, 1, 3).reshape(b, s, h * d)


def rotary(x):
    s, d = x.shape[2], x.shape[3]
    half = d // 2
    inv = ROPE_BASE ** (-jnp.arange(half, dtype=jnp.float32) * 2.0 / d)
    ang = jnp.arange(s, dtype=jnp.float32)[:, None] * inv[None, :]
    cos, sin = jnp.cos(ang), jnp.sin(ang)
    xf = x.astype(jnp.float32)
    x1, x2 = xf[..., :half], xf[..., half:]
    return jnp.concatenate([x1 * cos - x2 * sin, x1 * sin + x2 * cos], axis=-1).astype(x.dtype)


def stick_breaking_attention(q, k, v):
    s_len, d = q.shape[2], q.shape[3]
    scale = d ** -0.5
    outs = []
    for i in range(s_len // SB_BLOCK):
        t0 = i * SB_BLOCK
        n_keys = t0 + SB_BLOCK
        qb = q[:, :, t0:t0 + SB_BLOCK]
        kk = k[:, :, :n_keys]
        vv = v[:, :, :n_keys]
        z = jnp.einsum('bhtd,bhsd->bhts', qb, kk).astype(jnp.float32) * scale
        qpos = t0 + jnp.arange(SB_BLOCK)[:, None]
        kpos = jnp.arange(n_keys)[None, :]
        causal = kpos < qpos
        log_one_minus = jnp.where(causal, -jax.nn.softplus(z), 0.0)
        suffix = lax.cumsum(log_one_minus, axis=3, reverse=True) - log_one_minus
        log_a = jax.nn.log_sigmoid(z) + suffix
        a = jnp.where(causal, jnp.exp(log_a), 0.0).astype(v.dtype)
        outs.append(jnp.einsum('bhts,bhsd->bhtd', a, vv))
    return jnp.concatenate(outs, axis=2)


def retention_chunkwise(q, k, v):
    b, h, s_len, dk = q.shape
    dv = v.shape[3]
    c = RET_CHUNK
    n_chunks = s_len // c
    log_gamma = jnp.log1p(-jnp.exp2(-5.0 - jnp.arange(h, dtype=jnp.float32)))
    idx = jnp.arange(c, dtype=jnp.float32)
    diff = idx[:, None] - idx[None, :]
    d_local = jnp.where(diff >= 0.0,
                        jnp.exp(jnp.maximum(diff, 0.0)[None] * log_gamma[:, None, None]), 0.0)
    xi = jnp.exp((idx + 1.0)[None, :] * log_gamma[:, None])
    zeta = jnp.exp((c - 1.0 - idx)[None, :] * log_gamma[:, None])
    g_chunk = jnp.exp(c * log_gamma)

    def chunked(t):
        return jnp.moveaxis(t.reshape(b, h, n_chunks, c, t.shape[3]), 2, 0)

    def step(state, inp):
        qc, kc, vc = inp
        inner = jnp.einsum('bhid,bhjd->bhij', qc, kc) * d_local[None]
        y = (jnp.einsum('bhij,bhje->bhie', inner, vc)
             + jnp.einsum('bhid,bhde->bhie', qc, state) * xi[None, :, :, None])
        new_state = (state * g_chunk[None, :, None, None]
                     + jnp.einsum('bhjd,bhje->bhde', kc * zeta[None, :, :, None], vc))
        return new_state, y

    state0 = jnp.zeros((b, h, dk, dv), jnp.float32)
    _, ys = lax.scan(step, state0, (chunked(q), chunked(k), chunked(v)))
    return jnp.moveaxis(ys, 0, 2).reshape(b, h, s_len, dv).astype(v.dtype)


def causal_depthwise_conv(a, w, bias):
    s_len = a.shape[1]
    padded = jnp.pad(a, ((0, 0), (CONV_WIDTH - 1, 0), (0, 0)))
    out = bias
    for j in range(CONV_WIDTH):
        out = out + w[j] * padded[:, j:j + s_len]
    return out


def hybrid_layer(x, p_i, g_mix, w_in, sb_q_gain, sb_k_gain, ret_norm_gain,
                 w_branch_sb, w_branch_ret, w_out, g_ffn, w_up, conv_w, conv_b,
                 w_down, g_ple, w_ple_gate, w_ple_proj):
    h = rmsnorm(x, g_mix)
    q_sb, k_sb, v_sb, q_r, k_r, v_r, g_r, gate_sb, gate_ret = split_cols(h @ w_in)

    q_sb = head_rmsnorm(to_heads(q_sb, SB_HEADS), sb_q_gain)
    k_sb = head_rmsnorm(to_heads(k_sb, SB_HEADS), sb_k_gain)
    y_sb = from_heads(stick_breaking_attention(q_sb, k_sb, to_heads(v_sb, SB_HEADS)))

    q_r = rotary(to_heads(q_r, RET_HEADS))
    k_r = rotary(to_heads(k_r, RET_HEADS)) * (RET_QK_DIM ** -0.5)
    y_r = retention_chunkwise(q_r, k_r, to_heads(v_r, RET_HEADS))
    y_r = from_heads(head_rmsnorm(y_r, ret_norm_gain)) * jax.nn.silu(g_r)

    merged = (jax.nn.sigmoid(gate_sb) * (y_sb @ w_branch_sb)
              + jax.nn.sigmoid(gate_ret) * (y_r @ w_branch_ret))
    x = x + merged @ w_out

    h = rmsnorm(x, g_ffn)
    up = h @ w_up
    a, u = up[..., :D_FF], up[..., D_FF:]
    a = causal_depthwise_conv(a, conv_w, conv_b)
    x = x + (jax.nn.gelu(a) * u) @ w_down

    gate = jax.nn.sigmoid(rmsnorm(x, g_ple) @ w_ple_gate)
    x = x + gate * (p_i @ w_ple_proj)
    return x


def setup_inputs(seed: int = 0) -> dict:
    key = jax.random.key(seed)
    ks = jax.random.split(key, 20)

    def nrm(k, shape, fan_in):
        return jax.random.normal(k, shape, jnp.float32) * (fan_in ** -0.5)

    def gain(k, shape):
        return 1.0 + 0.1 * jax.random.normal(k, shape, jnp.float32)

    L = DEPTH
    return {
        "x": jax.random.normal(ks[0], (BATCH, SEQ, D_MODEL), jnp.float32),
        "p": jax.random.normal(ks[1], (DEPTH, BATCH, SEQ, PLE_DIM), jnp.float32),
        "g_mix": gain(ks[2], (L, D_MODEL)),
        "w_in": nrm(ks[3], (L, D_MODEL, D_IN), D_MODEL),
        "sb_q_gain": gain(ks[4], (L, SB_HEADS, SB_HEAD_DIM)),
        "sb_k_gain": gain(ks[5], (L, SB_HEADS, SB_HEAD_DIM)),
        "ret_norm_gain": gain(ks[6], (L, RET_HEADS, RET_V_DIM)),
        "w_branch_sb": nrm(ks[7], (L, SB_W, D_MODEL), SB_W),
        "w_branch_ret": nrm(ks[8], (L, RET_V_W, D_MODEL), RET_V_W),
        "w_out": nrm(ks[9], (L, D_MODEL, D_MODEL), D_MODEL),
        "g_ffn": gain(ks[10], (L, D_MODEL)),
        "w_up": nrm(ks[11], (L, D_MODEL, 2 * D_FF), D_MODEL),
        "conv_w": nrm(ks[12], (L, CONV_WIDTH, D_FF), CONV_WIDTH),
        "conv_b": 0.02 * jax.random.normal(ks[13], (L, D_FF), jnp.float32),
        "w_down": nrm(ks[14], (L, D_FF, D_MODEL), D_FF),
        "g_ple": gain(ks[15], (L, D_MODEL)),
        "w_ple_gate": nrm(ks[16], (L, D_MODEL, D_MODEL), D_MODEL),
        "w_ple_proj": nrm(ks[17], (L, PLE_DIM, D_MODEL), PLE_DIM),
    }


def reference(x, p, g_mix, w_in, sb_q_gain, sb_k_gain, ret_norm_gain, w_branch_sb,
              w_branch_ret, w_out, g_ffn, w_up, conv_w, conv_b, w_down, g_ple,
              w_ple_gate, w_ple_proj):
    for i in range(DEPTH):
        x = hybrid_layer(x, p[i], g_mix[i], w_in[i], sb_q_gain[i], sb_k_gain[i],
                         ret_norm_gain[i], w_branch_sb[i], w_branch_ret[i], w_out[i],
                         g_ffn[i], w_up[i], conv_w[i], conv_b[i], w_down[i], g_ple[i],
                         w_ple_gate[i], w_ple_proj[i])
    return x
```

```python
import functools

import jax
import jax.numpy as jnp
from jax import lax
from jax.experimental import pallas as pl
from jax.experimental.pallas import tpu as pltpu

D_MODEL = 1024
BATCH = 16
SEQ = 2048
PLE_DIM = 256
SB_HEADS = 8
SB_HEAD_DIM = 64
RET_HEADS = 8
RET_QK_DIM = 64
RET_V_DIM = 128
RET_CHUNK = 128
ROPE_BASE = 10000.0
D_FF = 2816
CONV_WIDTH = 3
EPS = 1e-6

SB_W = SB_HEADS * SB_HEAD_DIM
RET_QK_W = RET_HEADS * RET_QK_DIM
RET_V_W = RET_HEADS * RET_V_DIM
QKV_W = 3 * SB_W + 2 * RET_QK_W + RET_V_W
GATE_W = RET_V_W + 2 * D_MODEL

LANES = 128
SUBLANES = 8
ATT_BLOCK = 128
FF_CHUNK = 256
N_FF_CHUNKS = D_FF // FF_CHUNK
VMEM_LIMIT = 56 * 1024 * 1024

F32 = jnp.float32
BF16 = jnp.bfloat16


def _rmsnorm(x, gain):
    ms = jnp.mean(x * x, axis=-1, keepdims=True)
    return x * lax.rsqrt(ms + EPS) * gain


def _sigmoid(x):
    return 1.0 / (1.0 + jnp.exp(-x))


def _split_bf16(x):
    hi = x.astype(BF16)
    lo = (x - hi.astype(F32)).astype(BF16)
    return hi, lo


def _dot(a, b):
    return jnp.dot(a, b, preferred_element_type=F32)


def _dot_nt(a, b):
    return lax.dot_general(a, b, (((1,), (1,)), ((), ())), preferred_element_type=F32)


def _dot_tn(a, b):
    return lax.dot_general(a, b, (((0,), (0,)), ((), ())), preferred_element_type=F32)


def _group_mean_matrix(width, group):
    r = lax.broadcasted_iota(jnp.int32, (width, width), 0) // group
    c = lax.broadcasted_iota(jnp.int32, (width, width), 1) // group
    return jnp.where(r == c, 1.0 / group, 0.0).astype(BF16)


def _head_rmsnorm_cols(y, gain, gmat):
    width = gmat.shape[0]
    parts = []
    for c in range(y.shape[1] // width):
        sq = y[:, c * width:(c + 1) * width]
        sq = sq * sq
        hi, lo = _split_bf16(sq)
        parts.append(_dot(hi, gmat) + _dot(lo, gmat))
    ms = jnp.concatenate(parts, axis=1)
    return y * lax.rsqrt(ms + EPS) * gain


def _rotary_cols(y, cos, sin_signed, first_half):
    width = y.shape[1]
    half = RET_QK_DIM // 2
    up = pltpu.roll(y, width - half, axis=1)
    dn = pltpu.roll(y, half, axis=1)
    partner = jnp.where(first_half, up, dn)
    return y * cos + partner * sin_signed


def _proj_kernel(x_ref, g_ref, w_ref, qg_ref, kg_ref, cos_ref, sin_ref,
                 qsb_ref, ksb_ref, vsb_ref, qr_ref, kr_ref, vr_ref):
    h = _rmsnorm(x_ref[...], g_ref[...]).astype(BF16)
    gmat = _group_mean_matrix(2 * LANES, SB_HEAD_DIM)

    o = 0
    q = _dot(h, w_ref[:, o:o + SB_W]); o += SB_W
    qsb_ref[...] = _head_rmsnorm_cols(q, qg_ref[...], gmat).astype(BF16)
    k = _dot(h, w_ref[:, o:o + SB_W]); o += SB_W
    ksb_ref[...] = _head_rmsnorm_cols(k, kg_ref[...], gmat).astype(BF16)
    vsb_ref[...] = _dot(h, w_ref[:, o:o + SB_W]).astype(BF16); o += SB_W

    reps = RET_QK_W // LANES
    cos = jnp.concatenate([cos_ref[...]] * reps, axis=1)
    sin = jnp.concatenate([sin_ref[...]] * reps, axis=1)
    lane = lax.broadcasted_iota(jnp.int32, (1, RET_QK_W), 1)
    first_half = (lane % RET_QK_DIM) < (RET_QK_DIM // 2)
    q = _dot(h, w_ref[:, o:o + RET_QK_W]); o += RET_QK_W
    qr_ref[...] = _rotary_cols(q, cos, sin, first_half).astype(BF16)
    k = _dot(h, w_ref[:, o:o + RET_QK_W]); o += RET_QK_W
    kr_ref[...] = (_rotary_cols(k, cos, sin, first_half) * (RET_QK_DIM ** -0.5)).astype(BF16)
    vr_ref[...] = _dot(h, w_ref[:, o:o + RET_V_W]).astype(BF16)


def _proj(x2d, g_mix, w_qkv, q_gain, k_gain, cos_t, sin_t, tm):
    t = x2d.shape[0]
    tiles_per_seq = SEQ // tm
    row = lambda i: (i, 0)
    const = lambda i: (0, 0)
    pos = lambda i: (i % tiles_per_seq, 0)
    out_w = (SB_W, SB_W, SB_W, RET_QK_W, RET_QK_W, RET_V_W)
    return pl.pallas_call(
        _proj_kernel,
        grid=(t // tm,),
        in_specs=[
            pl.BlockSpec((tm, D_MODEL), row),
            pl.BlockSpec((1, D_MODEL), const),
            pl.BlockSpec((D_MODEL, QKV_W), const),
            pl.BlockSpec((1, SB_W), const),
            pl.BlockSpec((1, SB_W), const),
            pl.BlockSpec((tm, LANES), pos),
            pl.BlockSpec((tm, LANES), pos),
        ],
        out_specs=[pl.BlockSpec((tm, w), row) for w in out_w],
        out_shape=[jax.ShapeDtypeStruct((t, w), BF16) for w in out_w],
        compiler_params=pltpu.CompilerParams(
            dimension_semantics=("parallel",), vmem_limit_bytes=VMEM_LIMIT),
        name="proj",
    )(x2d, g_mix, w_qkv, q_gain, k_gain, cos_t, sin_t)


def _sb_kernel(q_ref, k_ref, v_ref, o_ref, vst_ref, carry_ref, acc_ref):
    blk = ATT_BLOCK
    n_blocks = SEQ // blk
    lane = lax.broadcasted_iota(jnp.int32, (1, LANES), 1)
    head0 = lane < SB_HEAD_DIM
    zero = jnp.zeros((), BF16)

    for j in range(n_blocks):
        vj = v_ref[0, j * blk:(j + 1) * blk, :]
        vst_ref[j, 0:blk, :] = jnp.where(head0, vj, zero)
        vst_ref[j, blk:2 * blk, :] = jnp.where(head0, zero, vj)

    r = lax.broadcasted_iota(jnp.int32, (blk, 2 * blk), 0)
    c = lax.broadcasted_iota(jnp.int32, (blk, 2 * blk), 1)
    cum_mat = jnp.where((r > c) | (c >= blk), 1.0, 0.0).astype(BF16)
    t_idx = lax.broadcasted_iota(jnp.int32, (2 * blk, blk), 0) % blk
    s_idx = lax.broadcasted_iota(jnp.int32, (2 * blk, blk), 1)
    causal = s_idx < t_idx

    def key_block(q2, j, diag):
        kj = k_ref[0, pl.ds(pl.multiple_of(j * blk, blk), blk), :]
        z = _dot_nt(q2, kj)
        sp = jnp.maximum(z, 0.0) + jnp.log(1.0 + jnp.exp(-jnp.abs(z)))
        if diag:
            sp = jnp.where(causal, sp, 0.0)
        hi, lo = _split_bf16(sp)
        sums = _dot(jnp.concatenate([hi, lo], axis=0), cum_mat)
        sums = sums[0:2 * blk] + sums[2 * blk:4 * blk]
        later = sums[:, 0:blk]
        total = sums[:, blk:2 * blk]
        a = jnp.exp(z - sp - carry_ref[...] - later)
        if diag:
            a = jnp.where(causal, a, 0.0)
        a = a.astype(BF16)
        a2 = jnp.concatenate([a[0:blk], a[blk:2 * blk]], axis=1)
        acc_ref[...] += _dot(a2, vst_ref[j])
        carry_ref[...] += total

    def query_block(i, _):
        q = q_ref[0, pl.ds(pl.multiple_of(i * blk, blk), blk), :]
        q2 = jnp.concatenate([jnp.where(head0, q, zero), jnp.where(head0, zero, q)], axis=0)
        carry_ref[...] = jnp.zeros_like(carry_ref)
        acc_ref[...] = jnp.zeros_like(acc_ref)
        key_block(q2, i, True)

        def earlier(n, _):
            key_block(q2, i - 1 - n, False)
            return 0

        lax.fori_loop(0, i, earlier, 0)
        o_ref[0, pl.ds(pl.multiple_of(i * blk, blk), blk), :] = acc_ref[...].astype(BF16)
        return 0

    lax.fori_loop(0, n_blocks, query_block, 0)


def _sb_attention(q, k, v):
    b = q.shape[0]
    pairs = SB_W // LANES
    spec = pl.BlockSpec((1, SEQ, LANES), lambda bi, hp: (bi, 0, hp))
    return pl.pallas_call(
        _sb_kernel,
        grid=(b, pairs),
        in_specs=[spec, spec, spec],
        out_specs=spec,
        out_shape=jax.ShapeDtypeStruct((b, SEQ, SB_W), BF16),
        scratch_shapes=[
            pltpu.VMEM((SEQ // ATT_BLOCK, 2 * ATT_BLOCK, LANES), BF16),
            pltpu.VMEM((2 * ATT_BLOCK, ATT_BLOCK), F32),
            pltpu.VMEM((ATT_BLOCK, LANES), F32),
        ],
        compiler_params=pltpu.CompilerParams(
            dimension_semantics=("parallel", "parallel"), vmem_limit_bytes=VMEM_LIMIT),
        name="sb_attn",
    )(q, k, v)


def _ret_kernel(lg_ref, q_ref, k_ref, v_ref, gain_ref, o_ref, state_ref):
    c = RET_CHUNK
    n_chunks = SEQ // c
    hp = pl.program_id(1)
    lane = lax.broadcasted_iota(jnp.int32, (1, LANES), 1)
    zero = jnp.zeros((), BF16)
    row = lax.broadcasted_iota(jnp.int32, (c, c), 0).astype(F32)
    col = lax.broadcasted_iota(jnp.int32, (c, c), 1).astype(F32)
    diff = row - col

    for hh in range(2):
        lg = lg_ref[2 * hp + hh]
        d_local = jnp.where(diff >= 0.0, jnp.exp(jnp.maximum(diff, 0.0) * lg), 0.0)
        xi = jnp.exp((row + 1.0) * lg)
        zeta = jnp.exp((c - 1.0 - row) * lg)
        g_chunk = jnp.exp(jnp.full((1, RET_V_DIM), c, F32) * lg)
        mine = (lane < RET_QK_DIM) if hh == 0 else (lane >= RET_QK_DIM)
        gain = gain_ref[:, hh * RET_V_DIM:(hh + 1) * RET_V_DIM]
        state_ref[...] = jnp.zeros_like(state_ref)
        for n in range(n_chunks):
            rows = slice(n * c, (n + 1) * c)
            qc = jnp.where(mine, q_ref[0, rows, :], zero)
            kc = k_ref[0, rows, :]
            vc = v_ref[0, rows, hh * RET_V_DIM:(hh + 1) * RET_V_DIM]
            state = state_ref[...]
            inner = _dot_nt(qc, kc) * d_local
            y = _dot(inner.astype(BF16), vc) + _dot(qc, state.astype(BF16)) * xi
            kz = jnp.where(mine, kc.astype(F32) * zeta, 0.0).astype(BF16)
            state_ref[...] = state * g_chunk + _dot_tn(kz, vc)
            ms = jnp.mean(y * y, axis=-1, keepdims=True)
            y = y * lax.rsqrt(ms + EPS) * gain
            o_ref[0, rows, hh * RET_V_DIM:(hh + 1) * RET_V_DIM] = y.astype(o_ref.dtype)


def _retention(log_gamma, q, k, v, gain):
    b = q.shape[0]
    pairs = RET_QK_W // LANES
    qk_spec = pl.BlockSpec((1, SEQ, LANES), lambda bi, hp, lg: (bi, 0, hp))
    v_spec = pl.BlockSpec((1, SEQ, 2 * RET_V_DIM), lambda bi, hp, lg: (bi, 0, hp))
    return pl.pallas_call(
        _ret_kernel,
        grid_spec=pltpu.PrefetchScalarGridSpec(
            num_scalar_prefetch=1,
            grid=(b, pairs),
            in_specs=[qk_spec, qk_spec, v_spec,
                      pl.BlockSpec((1, 2 * RET_V_DIM), lambda bi, hp, lg: (0, hp))],
            out_specs=v_spec,
            scratch_shapes=[pltpu.VMEM((LANES, RET_V_DIM), F32)],
        ),
        out_shape=jax.ShapeDtypeStruct((b, SEQ, RET_V_W), BF16),
        compiler_params=pltpu.CompilerParams(
            dimension_semantics=("parallel", "parallel"), vmem_limit_bytes=VMEM_LIMIT),
        name="ret",
    )(log_gamma, q, k, v, gain)


def _merge_kernel(x_ref, g_ref, wg_ref, ysb_ref, yr_ref, wsb_ref, wret_ref, wo_ref, o_ref):
    x = x_ref[...]
    h = _rmsnorm(x, g_ref[...]).astype(BF16)
    g_r = _dot(h, wg_ref[:, 0:RET_V_W])
    y_r = (yr_ref[...].astype(F32) * (g_r * _sigmoid(g_r))).astype(BF16)
    gate_sb = _dot(h, wg_ref[:, RET_V_W:RET_V_W + D_MODEL])
    merged = _sigmoid(gate_sb) * _dot(ysb_ref[...], wsb_ref[...])
    gate_ret = _dot(h, wg_ref[:, RET_V_W + D_MODEL:GATE_W])
    merged = merged + _sigmoid(gate_ret) * _dot(y_r, wret_ref[...])
    o_ref[...] = x + _dot(merged.astype(BF16), wo_ref[...])


def _merge(x2d, g_mix, w_gate, y_sb, y_r, w_sb, w_ret, w_out, tm):
    t = x2d.shape[0]
    row = lambda i: (i, 0)
    const = lambda i: (0, 0)
    return pl.pallas_call(
        _merge_kernel,
        grid=(t // tm,),
        in_specs=[
            pl.BlockSpec((tm, D_MODEL), row),
            pl.BlockSpec((1, D_MODEL), const),
            pl.BlockSpec((D_MODEL, GATE_W), const),
            pl.BlockSpec((tm, SB_W), row),
            pl.BlockSpec((tm, RET_V_W), row),
            pl.BlockSpec((SB_W, D_MODEL), const),
            pl.BlockSpec((RET_V_W, D_MODEL), const),
            pl.BlockSpec((D_MODEL, D_MODEL), const),
        ],
        out_specs=pl.BlockSpec((tm, D_MODEL), row),
        out_shape=jax.ShapeDtypeStruct((t, D_MODEL), F32),
        compiler_params=pltpu.CompilerParams(
            dimension_semantics=("parallel",), vmem_limit_bytes=VMEM_LIMIT),
        name="merge",
    )(x2d, g_mix, w_gate, y_sb, y_r, w_sb, w_ret, w_out)


def _gelu_tanh(x):
    return 0.5 * x * (1.0 + jnp.tanh(0.7978845608028654 * (x + 0.044715 * (x * x * x))))


def _ffn_kernel(tiles_per_seq, x_ref, p_ref, gf_ref, wa_ref, wu_ref, cw_ref, cb_ref, wd_ref,
                gp_ref, wpg_ref, wpp_ref, o_ref, halo_ref, acc_ref):
    tm = x_ref.shape[0]
    x = x_ref[...]
    h = _rmsnorm(x, gf_ref[...]).astype(BF16)
    acc_ref[...] = jnp.zeros_like(acc_ref)
    row8 = lax.broadcasted_iota(jnp.int32, (SUBLANES, FF_CHUNK), 0)

    @pl.when(pl.program_id(0) % tiles_per_seq == 0)
    def _():
        halo_ref[...] = jnp.zeros_like(halo_ref)

    def chunk(ci, _):
        a = _dot(h, wa_ref[ci])
        u = _dot(h, wu_ref[ci])
        prev = halo_ref[ci]
        halo_ref[ci] = a[tm - SUBLANES:tm]
        a1 = pltpu.roll(a, 1, axis=0)
        a2 = pltpu.roll(a, 2, axis=0)
        p1 = pltpu.roll(prev, 1, axis=0)
        p2 = pltpu.roll(prev, 2, axis=0)
        a1 = jnp.concatenate([jnp.where(row8 < 1, p1, a1[0:SUBLANES]), a1[SUBLANES:]], axis=0)
        a2 = jnp.concatenate([jnp.where(row8 < 2, p2, a2[0:SUBLANES]), a2[SUBLANES:]], axis=0)
        cw = cw_ref[ci]
        conv = cb_ref[ci] + cw[0:1] * a2 + cw[1:2] * a1 + cw[2:3] * a
        g = (_gelu_tanh(conv) * u).astype(BF16)
        acc_ref[...] += _dot(g, wd_ref[ci])
        return 0

    lax.fori_loop(0, N_FF_CHUNKS, chunk, 0)

    x2 = x + acc_ref[...]
    h3 = _rmsnorm(x2, gp_ref[...]).astype(BF16)
    gate = _sigmoid(_dot(h3, wpg_ref[...]))
    o_ref[...] = x2 + gate * _dot(p_ref[...].astype(BF16), wpp_ref[...])


def _ffn_ple(x2d, p2d, g_ffn, wa, wu, cw, cb, wd, g_ple, w_pg, w_pp, tm):
    t = x2d.shape[0]
    row = lambda i: (i, 0)
    c2 = lambda i: (0, 0)
    c3 = lambda i: (0, 0, 0)
    once = pl.Buffered(1)
    return pl.pallas_call(
        functools.partial(_ffn_kernel, SEQ // tm),
        grid=(t // tm,),
        in_specs=[
            pl.BlockSpec((tm, D_MODEL), row),
            pl.BlockSpec((tm, PLE_DIM), row),
            pl.BlockSpec((1, D_MODEL), c2),
            pl.BlockSpec((N_FF_CHUNKS, D_MODEL, FF_CHUNK), c3, pipeline_mode=once),
            pl.BlockSpec((N_FF_CHUNKS, D_MODEL, FF_CHUNK), c3, pipeline_mode=once),
            pl.BlockSpec((N_FF_CHUNKS, CONV_WIDTH, FF_CHUNK), c3),
            pl.BlockSpec((N_FF_CHUNKS, 1, FF_CHUNK), c3),
            pl.BlockSpec((N_FF_CHUNKS, FF_CHUNK, D_MODEL), c3, pipeline_mode=once),
            pl.BlockSpec((1, D_MODEL), c2),
            pl.BlockSpec((D_MODEL, D_MODEL), c2, pipeline_mode=once),
            pl.BlockSpec((PLE_DIM, D_MODEL), c2, pipeline_mode=once),
        ],
        out_specs=pl.BlockSpec((tm, D_MODEL), row),
        out_shape=jax.ShapeDtypeStruct((t, D_MODEL), F32),
        scratch_shapes=[
            pltpu.VMEM((N_FF_CHUNKS, SUBLANES, FF_CHUNK), F32),
            pltpu.VMEM((tm, D_MODEL), F32),
        ],
        compiler_params=pltpu.CompilerParams(
            dimension_semantics=("arbitrary",), vmem_limit_bytes=VMEM_LIMIT),
        name="ffn_ple",
    )(x2d, p2d, g_ffn, wa, wu, cw, cb, wd, g_ple, w_pg, w_pp)


def _rotary_tables():
    half = RET_QK_DIM // 2
    inv = ROPE_BASE ** (-jnp.arange(half, dtype=F32) * 2.0 / RET_QK_DIM)
    ang = jnp.arange(SEQ, dtype=F32)[:, None] * inv[None, :]
    reps = LANES // half
    cos = jnp.tile(jnp.cos(ang), (1, reps))
    sin = jnp.tile(jnp.sin(ang), (1, reps))
    first_half = (jnp.arange(LANES) % RET_QK_DIM) < half
    return cos, jnp.where(first_half[None, :], -sin, sin)


def _chunked_cols(w):
    k = w.shape[0]
    return w.reshape(k, N_FF_CHUNKS, FF_CHUNK).transpose(1, 0, 2)


def _layer(x, p_i, g_mix, w_in, sb_q_gain, sb_k_gain, ret_norm_gain, w_branch_sb,
           w_branch_ret, w_out, g_ffn, w_up, conv_w, conv_b, w_down, g_ple,
           w_ple_gate, w_ple_proj):
    b, s, d = x.shape
    t = b * s
    x2d = x.reshape(t, d)
    g_mix = g_mix.reshape(1, d)
    w_in = w_in.astype(BF16)
    cos_t, sin_t = _rotary_tables()
    q_gain = (sb_q_gain * (SB_HEAD_DIM ** -0.5)).reshape(1, SB_W)
    k_gain = sb_k_gain.reshape(1, SB_W)

    q_sb, k_sb, v_sb, q_r, k_r, v_r = _proj(
        x2d, g_mix, w_in[:, :QKV_W], q_gain, k_gain, cos_t, sin_t, tm=512)

    y_sb = _sb_attention(q_sb.reshape(b, s, SB_W), k_sb.reshape(b, s, SB_W),
                         v_sb.reshape(b, s, SB_W))

    log_gamma = jnp.log1p(-jnp.exp2(-5.0 - jnp.arange(RET_HEADS, dtype=F32)))
    y_r = _retention(log_gamma, q_r.reshape(b, s, RET_QK_W), k_r.reshape(b, s, RET_QK_W),
                     v_r.reshape(b, s, RET_V_W), ret_norm_gain.reshape(1, RET_V_W))

    x1 = _merge(x2d, g_mix, w_in[:, QKV_W:], y_sb.reshape(t, SB_W), y_r.reshape(t, RET_V_W),
                w_branch_sb.astype(BF16), w_branch_ret.astype(BF16), w_out.astype(BF16), tm=512)

    w_up = w_up.astype(BF16)
    out = _ffn_ple(
        x1, p_i.reshape(t, PLE_DIM), g_ffn.reshape(1, d),
        _chunked_cols(w_up[:, :D_FF]), _chunked_cols(w_up[:, D_FF:]),
        _chunked_cols(conv_w), _chunked_cols(conv_b.reshape(1, D_FF)),
        w_down.astype(BF16).reshape(N_FF_CHUNKS, FF_CHUNK, d),
        g_ple.reshape(1, d), w_ple_gate.astype(BF16), w_ple_proj.astype(BF16), tm=512)
    return out.reshape(b, s, d)


def kernel(x, p, g_mix, w_in, sb_q_gain, sb_k_gain, ret_norm_gain, w_branch_sb, w_branch_ret,
           w_out, g_ffn, w_up, conv_w, conv_b, w_down, g_ple, w_ple_gate, w_ple_proj):
    for i in range(p.shape[0]):
        x = _layer(x, p[i], g_mix[i], w_in[i], sb_q_gain[i], sb_k_gain[i], ret_norm_gain[i],
                   w_branch_sb[i], w_branch_ret[i], w_out[i], g_ffn[i], w_up[i], conv_w[i],
                   conv_b[i], w_down[i], g_ple[i], w_ple_gate[i], w_ple_proj[i])
    return x
```

```python
import functools

import jax
import jax.numpy as jnp
from jax import lax
from jax.experimental import pallas as pl
from jax.experimental.pallas import tpu as pltpu

D_MODEL = 1024
BATCH = 16
SEQ = 2048
PLE_DIM = 256
SB_HEADS = 8
SB_HEAD_DIM = 64
RET_HEADS = 8
RET_QK_DIM = 64
RET_V_DIM = 128
RET_CHUNK = 128
ROPE_BASE = 10000.0
D_FF = 2816
CONV_WIDTH = 3
EPS = 1e-6

SB_W = SB_HEADS * SB_HEAD_DIM
RET_QK_W = RET_HEADS * RET_QK_DIM
RET_V_W = RET_HEADS * RET_V_DIM
QKV_W = 3 * SB_W + 2 * RET_QK_W + RET_V_W
GATE_W = RET_V_W + 2 * D_MODEL

LANES = 128
SUBLANES = 8
ATT_BLOCK = 128
ATT_QTILE = 512
ATT_UNROLL = 2
ATT_ROW_PARTS = 2
SIGN_BIT = 0x80000000
FF_CHUNK = 256
N_FF_CHUNKS = D_FF // FF_CHUNK
VMEM_LIMIT = 56 * 1024 * 1024

F32 = jnp.float32
BF16 = jnp.bfloat16


def _rmsnorm(x, gain):
    ms = jnp.mean(x * x, axis=-1, keepdims=True)
    return x * lax.rsqrt(ms + EPS) * gain


def _sigmoid(x):
    return 1.0 / (1.0 + jnp.exp(-x))


def _split_bf16(x):
    hi = x.astype(BF16)
    lo = (x - hi.astype(F32)).astype(BF16)
    return hi, lo


def _dot(a, b):
    return jnp.dot(a, b, preferred_element_type=F32)


def _dot_nt(a, b):
    return lax.dot_general(a, b, (((1,), (1,)), ((), ())), preferred_element_type=F32)


def _dot_tn(a, b):
    return lax.dot_general(a, b, (((0,), (0,)), ((), ())), preferred_element_type=F32)


def _group_mean_matrix(width, group):
    r = lax.broadcasted_iota(jnp.int32, (width, width), 0) // group
    c = lax.broadcasted_iota(jnp.int32, (width, width), 1) // group
    return jnp.where(r == c, 1.0 / group, 0.0).astype(BF16)


def _head_rmsnorm_cols(y, gain, gmat):
    width = gmat.shape[0]
    parts = []
    for c in range(y.shape[1] // width):
        sq = y[:, c * width:(c + 1) * width]
        sq = sq * sq
        hi, lo = _split_bf16(sq)
        parts.append(_dot(hi, gmat) + _dot(lo, gmat))
    ms = jnp.concatenate(parts, axis=1)
    return y * lax.rsqrt(ms + EPS) * gain


def _rotary_cols(y, cos, sin_signed, first_half):
    width = y.shape[1]
    half = RET_QK_DIM // 2
    up = pltpu.roll(y, width - half, axis=1)
    dn = pltpu.roll(y, half, axis=1)
    partner = jnp.where(first_half, up, dn)
    return y * cos + partner * sin_signed


def _proj_kernel(x_ref, g_ref, w_ref, qg_ref, kg_ref, cos_ref, sin_ref,
                 qsb_ref, ksb_ref, vsb_ref, qr_ref, kr_ref, vr_ref):
    h = _rmsnorm(x_ref[...], g_ref[...]).astype(BF16)
    gmat = _group_mean_matrix(2 * LANES, SB_HEAD_DIM)

    o = 0
    q = _dot(h, w_ref[:, o:o + SB_W]); o += SB_W
    qsb_ref[...] = _head_rmsnorm_cols(q, qg_ref[...], gmat).astype(BF16)
    k = _dot(h, w_ref[:, o:o + SB_W]); o += SB_W
    ksb_ref[...] = _head_rmsnorm_cols(k, kg_ref[...], gmat).astype(BF16)
    vsb_ref[...] = _dot(h, w_ref[:, o:o + SB_W]).astype(BF16); o += SB_W

    reps = RET_QK_W // LANES
    cos = jnp.concatenate([cos_ref[...]] * reps, axis=1)
    sin = jnp.concatenate([sin_ref[...]] * reps, axis=1)
    lane = lax.broadcasted_iota(jnp.int32, (1, RET_QK_W), 1)
    first_half = (lane % RET_QK_DIM) < (RET_QK_DIM // 2)
    q = _dot(h, w_ref[:, o:o + RET_QK_W]); o += RET_QK_W
    qr_ref[...] = _rotary_cols(q, cos, sin, first_half).astype(BF16)
    k = _dot(h, w_ref[:, o:o + RET_QK_W]); o += RET_QK_W
    kr_ref[...] = (_rotary_cols(k, cos, sin, first_half) * (RET_QK_DIM ** -0.5)).astype(BF16)
    vr_ref[...] = _dot(h, w_ref[:, o:o + RET_V_W]).astype(BF16)


def _proj(x2d, g_mix, w_qkv, q_gain, k_gain, cos_t, sin_t, tm):
    t = x2d.shape[0]
    tiles_per_seq = SEQ // tm
    row = lambda i: (i, 0)
    const = lambda i: (0, 0)
    pos = lambda i: (i % tiles_per_seq, 0)
    out_w = (SB_W, SB_W, SB_W, RET_QK_W, RET_QK_W, RET_V_W)
    return pl.pallas_call(
        _proj_kernel,
        grid=(t // tm,),
        in_specs=[
            pl.BlockSpec((tm, D_MODEL), row),
            pl.BlockSpec((1, D_MODEL), const),
            pl.BlockSpec((D_MODEL, QKV_W), const),
            pl.BlockSpec((1, SB_W), const),
            pl.BlockSpec((1, SB_W), const),
            pl.BlockSpec((tm, LANES), pos),
            pl.BlockSpec((tm, LANES), pos),
        ],
        out_specs=[pl.BlockSpec((tm, w), row) for w in out_w],
        out_shape=[jax.ShapeDtypeStruct((t, w), BF16) for w in out_w],
        compiler_params=pltpu.CompilerParams(
            dimension_semantics=("parallel",), vmem_limit_bytes=VMEM_LIMIT),
        name="proj",
    )(x2d, g_mix, w_qkv, q_gain, k_gain, cos_t, sin_t)


def _sb_kernel(q_ref, k_ref, v_ref, o_ref, kst_ref, vst_ref, carry_ref, acc_ref):
    blk = ATT_BLOCK
    n_blocks = SEQ // blk
    qt = ATT_QTILE
    sub = qt // blk
    lane = lax.broadcasted_iota(jnp.int32, (1, LANES), 1)
    head0 = lane < SB_HEAD_DIM
    zero = jnp.zeros((), BF16)

    for j in range(n_blocks):
        kj = k_ref[0, j * blk:(j + 1) * blk, :]
        vj = v_ref[0, j * blk:(j + 1) * blk, :]
        kst_ref[j, 0:blk, :] = jnp.where(head0, kj, zero)
        kst_ref[j, blk:2 * blk, :] = jnp.where(head0, zero, kj)
        vst_ref[2 * j * blk:(2 * j + 1) * blk, :] = jnp.where(head0, vj, zero)
        vst_ref[(2 * j + 1) * blk:(2 * j + 2) * blk, :] = jnp.where(head0, zero, vj)

    r = lax.broadcasted_iota(jnp.int32, (2 * blk, 2 * blk), 0) % blk
    c = lax.broadcasted_iota(jnp.int32, (2 * blk, 2 * blk), 1)
    cum_mat = jnp.where((r > c) | (c >= blk), 1.0, 0.0).astype(BF16)
    t_idx = lax.broadcasted_iota(jnp.int32, (blk, 2 * blk), 0)
    s_idx = lax.broadcasted_iota(jnp.int32, (blk, 2 * blk), 1) % blk
    causal = s_idx < t_idx

    def mask_first_block(x):
        top = jnp.where(causal, x[0:blk], 0.0)
        return top if x.shape[0] == blk else jnp.concatenate([top, x[blk:]], axis=0)

    def scores(q, j):
        return _dot_nt(q, kst_ref[j])

    def cumulate(z, diag):
        neg_abs = pltpu.bitcast(pltpu.bitcast(z, jnp.uint32) | jnp.uint32(SIGN_BIT), F32)
        sp = jnp.maximum(z, 0.0) + jnp.log(1.0 + jnp.exp(neg_abs))
        if diag:
            sp = mask_first_block(sp)
        hi, lo = _split_bf16(sp)
        s0 = _dot(jnp.concatenate([hi[:, 0:blk], lo[:, 0:blk]], axis=1), cum_mat)
        s1 = _dot(jnp.concatenate([hi[:, blk:], lo[:, blk:]], axis=1), cum_mat)
        later = jnp.concatenate([s0[:, 0:blk], s1[:, 0:blk]], axis=1)
        total = jnp.concatenate([s0[:, blk:], s1[:, blk:]], axis=1)
        return z - sp, later, total

    def weights(w, later, carry, diag):
        a = jnp.exp(w - carry - later)
        if diag:
            a = mask_first_block(a)
        return a.astype(BF16)

    def key_blocks(q, j_low, n_keys, row_parts, diag):
        js = [j_low + u for u in reversed(range(n_keys))]
        zs = [[scores(q[r0:r1], j) for j in js] for (r0, r1) in row_parts]
        cs = [[cumulate(z, diag) for z in zp] for zp in zs]
        for (r0, r1), cp in zip(row_parts, cs):
            carry = carry_ref[r0:r1, :]
            a_parts = []
            for w, later, total in cp:
                a_parts.append(weights(w, later, carry, diag))
                carry = carry + total
            carry_ref[r0:r1, :] = carry
            a_cat = a_parts[0] if n_keys == 1 else jnp.concatenate(a_parts[::-1], axis=1)
            v_rows = pl.ds(pl.multiple_of(j_low * (2 * blk), 2 * blk), n_keys * 2 * blk)
            acc_ref[r0:r1, :] += _dot(a_cat, vst_ref[v_rows, :])

    part = qt // ATT_ROW_PARTS
    row_parts = [(p * part, (p + 1) * part) for p in range(ATT_ROW_PARTS)]

    def query_tile(ti, _):
        rows = pl.ds(pl.multiple_of(ti * qt, qt), qt)
        q = q_ref[0, rows, :]
        carry_ref[...] = jnp.zeros_like(carry_ref)
        acc_ref[...] = jnp.zeros_like(acc_ref)
        for jj in reversed(range(sub)):
            key_blocks(q, ti * sub + jj, 1, [(jj * blk, qt)], True)

        def earlier(n, _):
            key_blocks(q, ti * sub - (n + 1) * ATT_UNROLL, ATT_UNROLL, row_parts, False)
            return 0

        lax.fori_loop(0, ti * (sub // ATT_UNROLL), earlier, 0)
        o_ref[0, rows, :] = acc_ref[...].astype(BF16)
        return 0

    lax.fori_loop(0, SEQ // qt, query_tile, 0)


def _sb_attention(q, k, v):
    b = q.shape[0]
    pairs = SB_W // LANES
    spec = pl.BlockSpec((1, SEQ, LANES), lambda bi, hp: (bi, 0, hp))
    stacked = pltpu.VMEM((SEQ // ATT_BLOCK, 2 * ATT_BLOCK, LANES), BF16)
    return pl.pallas_call(
        _sb_kernel,
        grid=(b, pairs),
        in_specs=[spec, spec, spec],
        out_specs=spec,
        out_shape=jax.ShapeDtypeStruct((b, SEQ, SB_W), BF16),
        scratch_shapes=[
            stacked,
            pltpu.VMEM((2 * SEQ, LANES), BF16),
            pltpu.VMEM((ATT_QTILE, 2 * ATT_BLOCK), F32),
            pltpu.VMEM((ATT_QTILE, LANES), F32),
        ],
        compiler_params=pltpu.CompilerParams(
            dimension_semantics=("parallel", "parallel"), vmem_limit_bytes=VMEM_LIMIT),
        name="sb_attn",
    )(q, k, v)


def _ret_kernel(lg_ref, q_ref, k_ref, v_ref, gain_ref, o_ref, state_ref):
    c = RET_CHUNK
    n_chunks = SEQ // c
    hp = pl.program_id(1)
    lane = lax.broadcasted_iota(jnp.int32, (1, LANES), 1)
    zero = jnp.zeros((), BF16)
    row = lax.broadcasted_iota(jnp.int32, (c, c), 0).astype(F32)
    col = lax.broadcasted_iota(jnp.int32, (c, c), 1).astype(F32)
    diff = row - col

    for hh in range(2):
        lg = lg_ref[2 * hp + hh]
        d_local = jnp.where(diff >= 0.0, jnp.exp(jnp.maximum(diff, 0.0) * lg), 0.0)
        xi = jnp.exp((row + 1.0) * lg)
        zeta = jnp.exp((c - 1.0 - row) * lg)
        g_chunk = jnp.exp(jnp.full((1, RET_V_DIM), c, F32) * lg)
        mine = (lane < RET_QK_DIM) if hh == 0 else (lane >= RET_QK_DIM)
        gain = gain_ref[:, hh * RET_V_DIM:(hh + 1) * RET_V_DIM]
        state_ref[...] = jnp.zeros_like(state_ref)
        for n in range(n_chunks):
            rows = slice(n * c, (n + 1) * c)
            qc = jnp.where(mine, q_ref[0, rows, :], zero)
            kc = k_ref[0, rows, :]
            vc = v_ref[0, rows, hh * RET_V_DIM:(hh + 1) * RET_V_DIM]
            state = state_ref[...]
            inner = _dot_nt(qc, kc) * d_local
            y = _dot(inner.astype(BF16), vc) + _dot(qc, state.astype(BF16)) * xi
            kz = jnp.where(mine, kc.astype(F32) * zeta, 0.0).astype(BF16)
            state_ref[...] = state * g_chunk + _dot_tn(kz, vc)
            ms = jnp.mean(y * y, axis=-1, keepdims=True)
            y = y * lax.rsqrt(ms + EPS) * gain
            o_ref[0, rows, hh * RET_V_DIM:(hh + 1) * RET_V_DIM] = y.astype(o_ref.dtype)


def _retention(log_gamma, q, k, v, gain):
    b = q.shape[0]
    pairs = RET_QK_W // LANES
    qk_spec = pl.BlockSpec((1, SEQ, LANES), lambda bi, hp, lg: (bi, 0, hp))
    v_spec = pl.BlockSpec((1, SEQ, 2 * RET_V_DIM), lambda bi, hp, lg: (bi, 0, hp))
    return pl.pallas_call(
        _ret_kernel,
        grid_spec=pltpu.PrefetchScalarGridSpec(
            num_scalar_prefetch=1,
            grid=(b, pairs),
            in_specs=[qk_spec, qk_spec, v_spec,
                      pl.BlockSpec((1, 2 * RET_V_DIM), lambda bi, hp, lg: (0, hp))],
            out_specs=v_spec,
            scratch_shapes=[pltpu.VMEM((LANES, RET_V_DIM), F32)],
        ),
        out_shape=jax.ShapeDtypeStruct((b, SEQ, RET_V_W), BF16),
        compiler_params=pltpu.CompilerParams(
            dimension_semantics=("parallel", "parallel"), vmem_limit_bytes=VMEM_LIMIT),
        name="ret",
    )(log_gamma, q, k, v, gain)


def _merge_kernel(x_ref, g_ref, wg_ref, ysb_ref, yr_ref, wsb_ref, wret_ref, wo_ref, o_ref):
    x = x_ref[...]
    h = _rmsnorm(x, g_ref[...]).astype(BF16)
    g_r = _dot(h, wg_ref[:, 0:RET_V_W])
    y_r = (yr_ref[...].astype(F32) * (g_r * _sigmoid(g_r))).astype(BF16)
    gate_sb = _dot(h, wg_ref[:, RET_V_W:RET_V_W + D_MODEL])
    merged = _sigmoid(gate_sb) * _dot(ysb_ref[...], wsb_ref[...])
    gate_ret = _dot(h, wg_ref[:, RET_V_W + D_MODEL:GATE_W])
    merged = merged + _sigmoid(gate_ret) * _dot(y_r, wret_ref[...])
    o_ref[...] = x + _dot(merged.astype(BF16), wo_ref[...])


def _merge(x2d, g_mix, w_gate, y_sb, y_r, w_sb, w_ret, w_out, tm):
    t = x2d.shape[0]
    row = lambda i: (i, 0)
    const = lambda i: (0, 0)
    return pl.pallas_call(
        _merge_kernel,
        grid=(t // tm,),
        in_specs=[
            pl.BlockSpec((tm, D_MODEL), row),
            pl.BlockSpec((1, D_MODEL), const),
            pl.BlockSpec((D_MODEL, GATE_W), const),
            pl.BlockSpec((tm, SB_W), row),
            pl.BlockSpec((tm, RET_V_W), row),
            pl.BlockSpec((SB_W, D_MODEL), const),
            pl.BlockSpec((RET_V_W, D_MODEL), const),
            pl.BlockSpec((D_MODEL, D_MODEL), const),
        ],
        out_specs=pl.BlockSpec((tm, D_MODEL), row),
        out_shape=jax.ShapeDtypeStruct((t, D_MODEL), F32),
        compiler_params=pltpu.CompilerParams(
            dimension_semantics=("parallel",), vmem_limit_bytes=VMEM_LIMIT),
        name="merge",
    )(x2d, g_mix, w_gate, y_sb, y_r, w_sb, w_ret, w_out)


def _gelu_tanh(x):
    return 0.5 * x * (1.0 + jnp.tanh(0.7978845608028654 * (x + 0.044715 * (x * x * x))))


def _ffn_kernel(tiles_per_seq, x_ref, p_ref, gf_ref, wa_ref, wu_ref, cw_ref, cb_ref, wd_ref,
                gp_ref, wpg_ref, wpp_ref, o_ref, halo_ref, acc_ref):
    tm = x_ref.shape[0]
    x = x_ref[...]
    h = _rmsnorm(x, gf_ref[...]).astype(BF16)
    acc_ref[...] = jnp.zeros_like(acc_ref)
    row8 = lax.broadcasted_iota(jnp.int32, (SUBLANES, FF_CHUNK), 0)

    @pl.when(pl.program_id(0) % tiles_per_seq == 0)
    def _():
        halo_ref[...] = jnp.zeros_like(halo_ref)

    def chunk(ci, _):
        a = _dot(h, wa_ref[ci])
        u = _dot(h, wu_ref[ci])
        prev = halo_ref[ci]
        halo_ref[ci] = a[tm - SUBLANES:tm]
        a1 = pltpu.roll(a, 1, axis=0)
        a2 = pltpu.roll(a, 2, axis=0)
        p1 = pltpu.roll(prev, 1, axis=0)
        p2 = pltpu.roll(prev, 2, axis=0)
        a1 = jnp.concatenate([jnp.where(row8 < 1, p1, a1[0:SUBLANES]), a1[SUBLANES:]], axis=0)
        a2 = jnp.concatenate([jnp.where(row8 < 2, p2, a2[0:SUBLANES]), a2[SUBLANES:]], axis=0)
        cw = cw_ref[ci]
        conv = cb_ref[ci] + cw[0:1] * a2 + cw[1:2] * a1 + cw[2:3] * a
        g = (_gelu_tanh(conv) * u).astype(BF16)
        acc_ref[...] += _dot(g, wd_ref[ci])
        return 0

    lax.fori_loop(0, N_FF_CHUNKS, chunk, 0)

    x2 = x + acc_ref[...]
    h3 = _rmsnorm(x2, gp_ref[...]).astype(BF16)
    gate = _sigmoid(_dot(h3, wpg_ref[...]))
    o_ref[...] = x2 + gate * _dot(p_ref[...].astype(BF16), wpp_ref[...])


def _ffn_ple(x2d, p2d, g_ffn, wa, wu, cw, cb, wd, g_ple, w_pg, w_pp, tm):
    t = x2d.shape[0]
    row = lambda i: (i, 0)
    c2 = lambda i: (0, 0)
    c3 = lambda i: (0, 0, 0)
    once = pl.Buffered(1)
    return pl.pallas_call(
        functools.partial(_ffn_kernel, SEQ // tm),
        grid=(t // tm,),
        in_specs=[
            pl.BlockSpec((tm, D_MODEL), row),
            pl.BlockSpec((tm, PLE_DIM), row),
            pl.BlockSpec((1, D_MODEL), c2),
            pl.BlockSpec((N_FF_CHUNKS, D_MODEL, FF_CHUNK), c3, pipeline_mode=once),
            pl.BlockSpec((N_FF_CHUNKS, D_MODEL, FF_CHUNK), c3, pipeline_mode=once),
            pl.BlockSpec((N_FF_CHUNKS, CONV_WIDTH, FF_CHUNK), c3),
            pl.BlockSpec((N_FF_CHUNKS, 1, FF_CHUNK), c3),
            pl.BlockSpec((N_FF_CHUNKS, FF_CHUNK, D_MODEL), c3, pipeline_mode=once),
            pl.BlockSpec((1, D_MODEL), c2),
            pl.BlockSpec((D_MODEL, D_MODEL), c2, pipeline_mode=once),
            pl.BlockSpec((PLE_DIM, D_MODEL), c2, pipeline_mode=once),
        ],
        out_specs=pl.BlockSpec((tm, D_MODEL), row),
        out_shape=jax.ShapeDtypeStruct((t, D_MODEL), F32),
        scratch_shapes=[
            pltpu.VMEM((N_FF_CHUNKS, SUBLANES, FF_CHUNK), F32),
            pltpu.VMEM((tm, D_MODEL), F32),
        ],
        compiler_params=pltpu.CompilerParams(
            dimension_semantics=("arbitrary",), vmem_limit_bytes=VMEM_LIMIT),
        name="ffn_ple",
    )(x2d, p2d, g_ffn, wa, wu, cw, cb, wd, g_ple, w_pg, w_pp)


def _rotary_tables():
    half = RET_QK_DIM // 2
    inv = ROPE_BASE ** (-jnp.arange(half, dtype=F32) * 2.0 / RET_QK_DIM)
    ang = jnp.arange(SEQ, dtype=F32)[:, None] * inv[None, :]
    reps = LANES // half
    cos = jnp.tile(jnp.cos(ang), (1, reps))
    sin = jnp.tile(jnp.sin(ang), (1, reps))
    first_half = (jnp.arange(LANES) % RET_QK_DIM) < half
    return cos, jnp.where(first_half[None, :], -sin, sin)


def _chunked_cols(w):
    k = w.shape[0]
    return w.reshape(k, N_FF_CHUNKS, FF_CHUNK).transpose(1, 0, 2)


def _layer(x, p_i, g_mix, w_in, sb_q_gain, sb_k_gain, ret_norm_gain, w_branch_sb,
           w_branch_ret, w_out, g_ffn, w_up, conv_w, conv_b, w_down, g_ple,
           w_ple_gate, w_ple_proj):
    b, s, d = x.shape
    t = b * s
    x2d = x.reshape(t, d)
    g_mix = g_mix.reshape(1, d)
    w_in = w_in.astype(BF16)
    cos_t, sin_t = _rotary_tables()
    q_gain = (sb_q_gain * (SB_HEAD_DIM ** -0.5)).reshape(1, SB_W)
    k_gain = sb_k_gain.reshape(1, SB_W)

    q_sb, k_sb, v_sb, q_r, k_r, v_r = _proj(
        x2d, g_mix, w_in[:, :QKV_W], q_gain, k_gain, cos_t, sin_t, tm=512)

    y_sb = _sb_attention(q_sb.reshape(b, s, SB_W), k_sb.reshape(b, s, SB_W),
                         v_sb.reshape(b, s, SB_W))

    log_gamma = jnp.log1p(-jnp.exp2(-5.0 - jnp.arange(RET_HEADS, dtype=F32)))
    y_r = _retention(log_gamma, q_r.reshape(b, s, RET_QK_W), k_r.reshape(b, s, RET_QK_W),
                     v_r.reshape(b, s, RET_V_W), ret_norm_gain.reshape(1, RET_V_W))

    x1 = _merge(x2d, g_mix, w_in[:, QKV_W:], y_sb.reshape(t, SB_W), y_r.reshape(t, RET_V_W),
                w_branch_sb.astype(BF16), w_branch_ret.astype(BF16), w_out.astype(BF16), tm=512)

    w_up = w_up.astype(BF16)
    out = _ffn_ple(
        x1, p_i.reshape(t, PLE_DIM), g_ffn.reshape(1, d),
        _chunked_cols(w_up[:, :D_FF]), _chunked_cols(w_up[:, D_FF:]),
        _chunked_cols(conv_w), _chunked_cols(conv_b.reshape(1, D_FF)),
        w_down.astype(BF16).reshape(N_FF_CHUNKS, FF_CHUNK, d),
        g_ple.reshape(1, d), w_ple_gate.astype(BF16), w_ple_proj.astype(BF16), tm=512)
    return out.reshape(b, s, d)


def kernel(x, p, g_mix, w_in, sb_q_gain, sb_k_gain, ret_norm_gain, w_branch_sb, w_branch_ret,
           w_out, g_ffn, w_up, conv_w, conv_b, w_down, g_ple, w_ple_gate, w_ple_proj):
    for i in range(p.shape[0]):
        x = _layer(x, p[i], g_mix[i], w_in[i], sb_q_gain[i], sb_k_gain[i], ret_norm_gain[i],
                   w_branch_sb[i], w_branch_ret[i], w_out[i], g_ffn[i], w_up[i], conv_w[i],
                   conv_b[i], w_down[i], g_ple[i], w_ple_gate[i], w_ple_proj[i])
    return x
```

```python
import functools

import jax
import jax.numpy as jnp
from jax import lax
from jax.experimental import pallas as pl
from jax.experimental.pallas import tpu as pltpu

D_MODEL = 1024
BATCH = 16
SEQ = 2048
PLE_DIM = 256
SB_HEADS = 8
SB_HEAD_DIM = 64
RET_HEADS = 8
RET_QK_DIM = 64
RET_V_DIM = 128
RET_CHUNK = 128
ROPE_BASE = 10000.0
D_FF = 2816
CONV_WIDTH = 3
EPS = 1e-6

SB_W = SB_HEADS * SB_HEAD_DIM
RET_QK_W = RET_HEADS * RET_QK_DIM
RET_V_W = RET_HEADS * RET_V_DIM
QKV_W = 3 * SB_W + 2 * RET_QK_W + RET_V_W
GATE_W = RET_V_W + 2 * D_MODEL

LANES = 128
SUBLANES = 8
ATT_BLOCK = 128
ATT_QTILE = 512
ATT_UNROLL = 2
ATT_ROW_PARTS = 2
FF_CHUNK = 256
N_FF_CHUNKS = D_FF // FF_CHUNK
assert N_FF_CHUNKS * FF_CHUNK == D_FF and N_FF_CHUNKS % 2 == 1
VMEM_LIMIT = 56 * 1024 * 1024

F32 = jnp.float32
BF16 = jnp.bfloat16


def _rmsnorm(x, gain):
    ms = jnp.mean(x * x, axis=-1, keepdims=True)
    return x * lax.rsqrt(ms + EPS) * gain


def _sigmoid(x):
    return 1.0 / (1.0 + jnp.exp(-x))


def _split_bf16(x):
    hi = x.astype(BF16)
    lo = (x - hi.astype(F32)).astype(BF16)
    return hi, lo


def _dot(a, b):
    return jnp.dot(a, b, preferred_element_type=F32)


def _dot_nt(a, b):
    return lax.dot_general(a, b, (((1,), (1,)), ((), ())), preferred_element_type=F32)


def _dot_tn(a, b):
    return lax.dot_general(a, b, (((0,), (0,)), ((), ())), preferred_element_type=F32)


def _group_mean_matrix(width, group):
    r = lax.broadcasted_iota(jnp.int32, (width, width), 0) // group
    c = lax.broadcasted_iota(jnp.int32, (width, width), 1) // group
    return jnp.where(r == c, 1.0 / group, 0.0).astype(BF16)


def _head_rmsnorm_cols(y, gain, gmat):
    width = gmat.shape[0]
    parts = []
    for c in range(y.shape[1] // width):
        sq = y[:, c * width:(c + 1) * width]
        sq = sq * sq
        hi, lo = _split_bf16(sq)
        parts.append(_dot(hi, gmat) + _dot(lo, gmat))
    ms = jnp.concatenate(parts, axis=1)
    return y * lax.rsqrt(ms + EPS) * gain


def _rotary_cols(y, cos, sin_signed, first_half):
    width = y.shape[1]
    half = RET_QK_DIM // 2
    up = pltpu.roll(y, width - half, axis=1)
    dn = pltpu.roll(y, half, axis=1)
    partner = jnp.where(first_half, up, dn)
    return y * cos + partner * sin_signed


def _proj_kernel(x_ref, g_ref, w_ref, qg_ref, kg_ref, cos_ref, sin_ref,
                 qsb_ref, ksb_ref, vsb_ref, qr_ref, kr_ref, vr_ref):
    h = _rmsnorm(x_ref[...], g_ref[...]).astype(BF16)
    gmat = _group_mean_matrix(2 * LANES, SB_HEAD_DIM)

    o = 0
    q = _dot(h, w_ref[:, o:o + SB_W]); o += SB_W
    qsb_ref[...] = _head_rmsnorm_cols(q, qg_ref[...], gmat).astype(BF16)
    k = _dot(h, w_ref[:, o:o + SB_W]); o += SB_W
    ksb_ref[...] = _head_rmsnorm_cols(k, kg_ref[...], gmat).astype(BF16)
    vsb_ref[...] = _dot(h, w_ref[:, o:o + SB_W]).astype(BF16); o += SB_W

    reps = RET_QK_W // LANES
    cos = jnp.concatenate([cos_ref[...]] * reps, axis=1)
    sin = jnp.concatenate([sin_ref[...]] * reps, axis=1)
    lane = lax.broadcasted_iota(jnp.int32, (1, RET_QK_W), 1)
    first_half = (lane % RET_QK_DIM) < (RET_QK_DIM // 2)
    q = _dot(h, w_ref[:, o:o + RET_QK_W]); o += RET_QK_W
    qr_ref[...] = _rotary_cols(q, cos, sin, first_half).astype(BF16)
    k = _dot(h, w_ref[:, o:o + RET_QK_W]); o += RET_QK_W
    kr_ref[...] = (_rotary_cols(k, cos, sin, first_half) * (RET_QK_DIM ** -0.5)).astype(BF16)
    vr_ref[...] = _dot(h, w_ref[:, o:o + RET_V_W]).astype(BF16)


def _proj(x2d, g_mix, w_qkv, q_gain, k_gain, cos_t, sin_t, tm):
    t = x2d.shape[0]
    tiles_per_seq = SEQ // tm
    row = lambda i: (i, 0)
    const = lambda i: (0, 0)
    pos = lambda i: (i % tiles_per_seq, 0)
    out_w = (SB_W, SB_W, SB_W, RET_QK_W, RET_QK_W, RET_V_W)
    return pl.pallas_call(
        _proj_kernel,
        grid=(t // tm,),
        in_specs=[
            pl.BlockSpec((tm, D_MODEL), row),
            pl.BlockSpec((1, D_MODEL), const),
            pl.BlockSpec((D_MODEL, QKV_W), const),
            pl.BlockSpec((1, SB_W), const),
            pl.BlockSpec((1, SB_W), const),
            pl.BlockSpec((tm, LANES), pos),
            pl.BlockSpec((tm, LANES), pos),
        ],
        out_specs=[pl.BlockSpec((tm, w), row) for w in out_w],
        out_shape=[jax.ShapeDtypeStruct((t, w), BF16) for w in out_w],
        compiler_params=pltpu.CompilerParams(
            dimension_semantics=("parallel",), vmem_limit_bytes=VMEM_LIMIT),
        name="proj",
    )(x2d, g_mix, w_qkv, q_gain, k_gain, cos_t, sin_t)


def _sb_kernel(q_ref, k_ref, v_ref, o_ref, kst_ref, vst_ref, carry_ref, acc_ref):
    blk = ATT_BLOCK
    n_blocks = SEQ // blk
    qt = ATT_QTILE
    sub = qt // blk
    lane = lax.broadcasted_iota(jnp.int32, (1, LANES), 1)
    head0 = lane < SB_HEAD_DIM
    zero = jnp.zeros((), BF16)

    for j in range(n_blocks):
        kj = k_ref[0, j * blk:(j + 1) * blk, :]
        vj = v_ref[0, j * blk:(j + 1) * blk, :]
        kst_ref[j, 0:blk, :] = jnp.where(head0, kj, zero)
        kst_ref[j, blk:2 * blk, :] = jnp.where(head0, zero, kj)
        vst_ref[2 * j * blk:(2 * j + 1) * blk, :] = jnp.where(head0, vj, zero)
        vst_ref[(2 * j + 1) * blk:(2 * j + 2) * blk, :] = jnp.where(head0, zero, vj)

    r = lax.broadcasted_iota(jnp.int32, (2 * blk, 2 * blk), 0) % blk
    c = lax.broadcasted_iota(jnp.int32, (2 * blk, 2 * blk), 1)
    cum_mat = jnp.where((r > c) | (c >= blk), 1.0, 0.0).astype(BF16)
    t_idx = lax.broadcasted_iota(jnp.int32, (blk, 2 * blk), 0)
    s_idx = lax.broadcasted_iota(jnp.int32, (blk, 2 * blk), 1) % blk
    causal = s_idx < t_idx

    def mask_first_block(x):
        top = jnp.where(causal, x[0:blk], 0.0)
        return top if x.shape[0] == blk else jnp.concatenate([top, x[blk:]], axis=0)

    def scores(q, j):
        return _dot_nt(q, kst_ref[j])

    def cumulate(z, diag):
        sp = jnp.maximum(z, 0.0) + jnp.log(1.0 + jnp.exp(-jnp.abs(z)))
        if diag:
            sp = mask_first_block(sp)
        hi, lo = _split_bf16(sp)
        s0 = _dot(jnp.concatenate([hi[:, 0:blk], lo[:, 0:blk]], axis=1), cum_mat)
        s1 = _dot(jnp.concatenate([hi[:, blk:], lo[:, blk:]], axis=1), cum_mat)
        later = jnp.concatenate([s0[:, 0:blk], s1[:, 0:blk]], axis=1)
        total = jnp.concatenate([s0[:, blk:], s1[:, blk:]], axis=1)
        return z - sp, later, total

    def weights(w, later, carry, diag):
        a = jnp.exp(w - carry - later)
        if diag:
            a = mask_first_block(a)
        return a.astype(BF16)

    def key_blocks(q, j_low, n_keys, row_parts, diag):
        js = [j_low + u for u in reversed(range(n_keys))]
        zs = [[scores(q[r0:r1], j) for j in js] for (r0, r1) in row_parts]
        cs = [[cumulate(z, diag) for z in zp] for zp in zs]
        for (r0, r1), cp in zip(row_parts, cs):
            carry = carry_ref[r0:r1, :]
            a_parts = []
            for w, later, total in cp:
                a_parts.append(weights(w, later, carry, diag))
                carry = carry + total
            carry_ref[r0:r1, :] = carry
            a_cat = a_parts[0] if n_keys == 1 else jnp.concatenate(a_parts[::-1], axis=1)
            v_rows = pl.ds(pl.multiple_of(j_low * (2 * blk), 2 * blk), n_keys * 2 * blk)
            acc_ref[r0:r1, :] += _dot(a_cat, vst_ref[v_rows, :])

    part = qt // ATT_ROW_PARTS
    row_parts = [(p * part, (p + 1) * part) for p in range(ATT_ROW_PARTS)]

    def query_tile(ti, _):
        rows = pl.ds(pl.multiple_of(ti * qt, qt), qt)
        q = q_ref[0, rows, :]
        carry_ref[...] = jnp.zeros_like(carry_ref)
        acc_ref[...] = jnp.zeros_like(acc_ref)
        for jj in reversed(range(sub)):
            key_blocks(q, ti * sub + jj, 1, [(jj * blk, qt)], True)

        def earlier(n, _):
            key_blocks(q, ti * sub - (n + 1) * ATT_UNROLL, ATT_UNROLL, row_parts, False)
            return 0

        lax.fori_loop(0, ti * (sub // ATT_UNROLL), earlier, 0)
        o_ref[0, rows, :] = acc_ref[...].astype(BF16)
        return 0

    lax.fori_loop(0, SEQ // qt, query_tile, 0)


def _sb_attention(q, k, v):
    b = q.shape[0]
    pairs = SB_W // LANES
    spec = pl.BlockSpec((1, SEQ, LANES), lambda bi, hp: (bi, 0, hp))
    stacked = pltpu.VMEM((SEQ // ATT_BLOCK, 2 * ATT_BLOCK, LANES), BF16)
    return pl.pallas_call(
        _sb_kernel,
        grid=(b, pairs),
        in_specs=[spec, spec, spec],
        out_specs=spec,
        out_shape=jax.ShapeDtypeStruct((b, SEQ, SB_W), BF16),
        scratch_shapes=[
            stacked,
            pltpu.VMEM((2 * SEQ, LANES), BF16),
            pltpu.VMEM((ATT_QTILE, 2 * ATT_BLOCK), F32),
            pltpu.VMEM((ATT_QTILE, LANES), F32),
        ],
        compiler_params=pltpu.CompilerParams(
            dimension_semantics=("parallel", "parallel"), vmem_limit_bytes=VMEM_LIMIT),
        name="sb_attn",
    )(q, k, v)


def _ret_kernel(lg_ref, q_ref, k_ref, v_ref, gain_ref, o_ref, kst_ref, vbd_ref, state_ref):
    c = RET_CHUNK
    dv = RET_V_DIM
    n_chunks = SEQ // c
    hp = pl.program_id(1)
    zero = jnp.zeros((), BF16)
    qk_head0 = lax.broadcasted_iota(jnp.int32, (1, LANES), 1) < RET_QK_DIM
    col_head0 = lax.broadcasted_iota(jnp.int32, (1, 2 * dv), 1) < dv
    lg0 = lg_ref[2 * hp]
    lg1 = lg_ref[2 * hp + 1]
    lg_col = jnp.where(col_head0, lg0, lg1)
    lg_qk = jnp.where(qk_head0, lg0, lg1)

    row = lax.broadcasted_iota(jnp.int32, (c, 2 * c), 0).astype(F32)
    col = (lax.broadcasted_iota(jnp.int32, (c, 2 * c), 1) % c).astype(F32)
    diff = row - col
    d_local = jnp.where(diff >= 0.0, jnp.exp(jnp.maximum(diff, 0.0) * lg_col), 0.0)
    xi = jnp.exp((row + 1.0) * lg_col)
    g_chunk = jnp.exp(jnp.full((1, 2 * dv), c, F32) * lg_col)
    row_k = lax.broadcasted_iota(jnp.int32, (c, LANES), 0).astype(F32)
    zeta = jnp.exp((c - 1.0 - row_k) * lg_qk)
    own_block = ((lax.broadcasted_iota(jnp.int32, (LANES, 2 * dv), 0) < RET_QK_DIM)
                 == (lax.broadcasted_iota(jnp.int32, (LANES, 2 * dv), 1) < dv))

    for n in range(n_chunks):
        rows = slice(n * c, (n + 1) * c)
        kn = k_ref[0, rows, :]
        vn = v_ref[0, rows, :]
        kst_ref[n, 0:c, :] = jnp.where(qk_head0, kn, zero)
        kst_ref[n, c:2 * c, :] = jnp.where(qk_head0, zero, kn)
        vbd_ref[n, 0:c, :] = jnp.where(col_head0, vn, zero)
        vbd_ref[n, c:2 * c, :] = jnp.where(col_head0, zero, vn)

    state_ref[...] = jnp.zeros_like(state_ref)
    for n in range(n_chunks):
        rows = slice(n * c, (n + 1) * c)
        qc = q_ref[0, rows, :]
        state = state_ref[...]
        inner = _dot_nt(qc, kst_ref[n]) * d_local
        y = _dot(inner.astype(BF16), vbd_ref[n]) + _dot(qc, state.astype(BF16)) * xi
        kz = (k_ref[0, rows, :].astype(F32) * zeta).astype(BF16)
        update = _dot_tn(kz, v_ref[0, rows, :])
        state_ref[...] = state * g_chunk + jnp.where(own_block, update, 0.0)
        for hh in range(2):
            yh = y[:, hh * dv:(hh + 1) * dv]
            ms = jnp.mean(yh * yh, axis=-1, keepdims=True)
            yh = yh * lax.rsqrt(ms + EPS) * gain_ref[:, hh * dv:(hh + 1) * dv]
            o_ref[0, rows, hh * dv:(hh + 1) * dv] = yh.astype(o_ref.dtype)


def _retention(log_gamma, q, k, v, gain):
    b = q.shape[0]
    pairs = RET_QK_W // LANES
    qk_spec = pl.BlockSpec((1, SEQ, LANES), lambda bi, hp, lg: (bi, 0, hp))
    v_spec = pl.BlockSpec((1, SEQ, 2 * RET_V_DIM), lambda bi, hp, lg: (bi, 0, hp))
    return pl.pallas_call(
        _ret_kernel,
        grid_spec=pltpu.PrefetchScalarGridSpec(
            num_scalar_prefetch=1,
            grid=(b, pairs),
            in_specs=[qk_spec, qk_spec, v_spec,
                      pl.BlockSpec((1, 2 * RET_V_DIM), lambda bi, hp, lg: (0, hp))],
            out_specs=v_spec,
            scratch_shapes=[
                pltpu.VMEM((SEQ // RET_CHUNK, 2 * RET_CHUNK, LANES), BF16),
                pltpu.VMEM((SEQ // RET_CHUNK, 2 * RET_CHUNK, 2 * RET_V_DIM), BF16),
                pltpu.VMEM((LANES, 2 * RET_V_DIM), F32),
            ],
        ),
        out_shape=jax.ShapeDtypeStruct((b, SEQ, RET_V_W), BF16),
        compiler_params=pltpu.CompilerParams(
            dimension_semantics=("parallel", "parallel"), vmem_limit_bytes=VMEM_LIMIT),
        name="ret",
    )(log_gamma, q, k, v, gain)


def _merge_kernel(x_ref, g_ref, wg_ref, ysb_ref, yr_ref, wsb_ref, wret_ref, wo_ref, o_ref):
    x = x_ref[...]
    h = _rmsnorm(x, g_ref[...]).astype(BF16)
    g_r = _dot(h, wg_ref[:, 0:RET_V_W])
    y_r = (yr_ref[...].astype(F32) * (g_r * _sigmoid(g_r))).astype(BF16)
    gate_sb = _dot(h, wg_ref[:, RET_V_W:RET_V_W + D_MODEL])
    merged = _sigmoid(gate_sb) * _dot(ysb_ref[...], wsb_ref[...])
    gate_ret = _dot(h, wg_ref[:, RET_V_W + D_MODEL:GATE_W])
    merged = merged + _sigmoid(gate_ret) * _dot(y_r, wret_ref[...])
    o_ref[...] = x + _dot(merged.astype(BF16), wo_ref[...])


def _merge(x2d, g_mix, w_gate, y_sb, y_r, w_sb, w_ret, w_out, tm):
    t = x2d.shape[0]
    row = lambda i: (i, 0)
    const = lambda i: (0, 0)
    return pl.pallas_call(
        _merge_kernel,
        grid=(t // tm,),
        in_specs=[
            pl.BlockSpec((tm, D_MODEL), row),
            pl.BlockSpec((1, D_MODEL), const),
            pl.BlockSpec((D_MODEL, GATE_W), const),
            pl.BlockSpec((tm, SB_W), row),
            pl.BlockSpec((tm, RET_V_W), row),
            pl.BlockSpec((SB_W, D_MODEL), const),
            pl.BlockSpec((RET_V_W, D_MODEL), const),
            pl.BlockSpec((D_MODEL, D_MODEL), const),
        ],
        out_specs=pl.BlockSpec((tm, D_MODEL), row),
        out_shape=jax.ShapeDtypeStruct((t, D_MODEL), F32),
        compiler_params=pltpu.CompilerParams(
            dimension_semantics=("parallel",), vmem_limit_bytes=VMEM_LIMIT),
        name="merge",
    )(x2d, g_mix, w_gate, y_sb, y_r, w_sb, w_ret, w_out)


def _gelu_tanh(x):
    return 0.5 * x * (1.0 + jnp.tanh(0.7978845608028654 * (x + 0.044715 * (x * x * x))))


def _ffn_kernel(tiles_per_seq, x_ref, p_ref, gf_ref, wa_ref, wu_ref, cw_ref, cb_ref, wd_ref,
                gp_ref, wpg_ref, wpp_ref, o_ref, halo_ref, h_ref, au_ref, g_ref):
    tm = x_ref.shape[0]
    h_ref[...] = _rmsnorm(x_ref[...], gf_ref[...]).astype(BF16)
    row8 = lax.broadcasted_iota(jnp.int32, (SUBLANES, FF_CHUNK), 0)

    @pl.when(pl.program_id(0) % tiles_per_seq == 0)
    def _():
        halo_ref[...] = jnp.zeros_like(halo_ref)

    def up(ci, slot):
        h = h_ref[...]
        au_ref[slot, 0] = _dot(h, wa_ref[ci])
        au_ref[slot, 1] = _dot(h, wu_ref[ci])

    def gated(ci, slot):
        a = au_ref[slot, 0]
        u = au_ref[slot, 1]
        prev = halo_ref[ci]
        halo_ref[ci] = a[tm - SUBLANES:tm]
        a1 = pltpu.roll(a, 1, axis=0)
        a2 = pltpu.roll(a, 2, axis=0)
        p1 = pltpu.roll(prev, 1, axis=0)
        p2 = pltpu.roll(prev, 2, axis=0)
        a1 = jnp.concatenate([jnp.where(row8 < 1, p1, a1[0:SUBLANES]), a1[SUBLANES:]], axis=0)
        a2 = jnp.concatenate([jnp.where(row8 < 2, p2, a2[0:SUBLANES]), a2[SUBLANES:]], axis=0)
        cw = cw_ref[ci]
        conv = cb_ref[ci] + cw[0:1] * a2 + cw[1:2] * a1 + cw[2:3] * a
        cols = pl.ds(pl.multiple_of(ci * FF_CHUNK, FF_CHUNK), FF_CHUNK)
        g_ref[:, cols] = (_gelu_tanh(conv) * u).astype(BF16)

    up(0, 0)

    def chunk_pair(k, _):
        up(2 * k + 1, 1)
        gated(2 * k, 0)
        up(2 * k + 2, 0)
        gated(2 * k + 1, 1)
        return 0

    lax.fori_loop(0, (N_FF_CHUNKS - 1) // 2, chunk_pair, 0)
    emb = _dot(p_ref[...].astype(BF16), wpp_ref[...])
    gated(N_FF_CHUNKS - 1, (N_FF_CHUNKS - 1) % 2)

    x2 = x_ref[...] + _dot(g_ref[...], wd_ref[...])
    h3 = _rmsnorm(x2, gp_ref[...]).astype(BF16)
    gate = _sigmoid(_dot(h3, wpg_ref[...]))
    o_ref[...] = x2 + gate * emb


def _ffn_ple(x2d, p2d, g_ffn, wa, wu, cw, cb, wd, g_ple, w_pg, w_pp, tm):
    t = x2d.shape[0]
    row = lambda i: (i, 0)
    c2 = lambda i: (0, 0)
    c3 = lambda i: (0, 0, 0)
    once = pl.Buffered(1)
    return pl.pallas_call(
        functools.partial(_ffn_kernel, SEQ // tm),
        grid=(t // tm,),
        in_specs=[
            pl.BlockSpec((tm, D_MODEL), row),
            pl.BlockSpec((tm, PLE_DIM), row),
            pl.BlockSpec((1, D_MODEL), c2),
            pl.BlockSpec((N_FF_CHUNKS, D_MODEL, FF_CHUNK), c3, pipeline_mode=once),
            pl.BlockSpec((N_FF_CHUNKS, D_MODEL, FF_CHUNK), c3, pipeline_mode=once),
            pl.BlockSpec((N_FF_CHUNKS, CONV_WIDTH, FF_CHUNK), c3),
            pl.BlockSpec((N_FF_CHUNKS, 1, FF_CHUNK), c3),
            pl.BlockSpec((D_FF, D_MODEL), c2, pipeline_mode=once),
            pl.BlockSpec((1, D_MODEL), c2),
            pl.BlockSpec((D_MODEL, D_MODEL), c2, pipeline_mode=once),
            pl.BlockSpec((PLE_DIM, D_MODEL), c2, pipeline_mode=once),
        ],
        out_specs=pl.BlockSpec((tm, D_MODEL), row),
        out_shape=jax.ShapeDtypeStruct((t, D_MODEL), F32),
        scratch_shapes=[
            pltpu.VMEM((N_FF_CHUNKS, SUBLANES, FF_CHUNK), F32),
            pltpu.VMEM((tm, D_MODEL), BF16),
            pltpu.VMEM((2, 2, tm, FF_CHUNK), F32),
            pltpu.VMEM((tm, D_FF), BF16),
        ],
        compiler_params=pltpu.CompilerParams(
            dimension_semantics=("arbitrary",), vmem_limit_bytes=VMEM_LIMIT),
        name="ffn_ple",
    )(x2d, p2d, g_ffn, wa, wu, cw, cb, wd, g_ple, w_pg, w_pp)


def _rotary_tables():
    half = RET_QK_DIM // 2
    inv = ROPE_BASE ** (-jnp.arange(half, dtype=F32) * 2.0 / RET_QK_DIM)
    ang = jnp.arange(SEQ, dtype=F32)[:, None] * inv[None, :]
    reps = LANES // half
    cos = jnp.tile(jnp.cos(ang), (1, reps))
    sin = jnp.tile(jnp.sin(ang), (1, reps))
    first_half = (jnp.arange(LANES) % RET_QK_DIM) < half
    return cos, jnp.where(first_half[None, :], -sin, sin)


def _chunked_cols(w):
    k = w.shape[0]
    return w.reshape(k, N_FF_CHUNKS, FF_CHUNK).transpose(1, 0, 2)


def _layer(x, p_i, g_mix, w_in, sb_q_gain, sb_k_gain, ret_norm_gain, w_branch_sb,
           w_branch_ret, w_out, g_ffn, w_up, conv_w, conv_b, w_down, g_ple,
           w_ple_gate, w_ple_proj):
    b, s, d = x.shape
    t = b * s
    x2d = x.reshape(t, d)
    g_mix = g_mix.reshape(1, d)
    w_in = w_in.astype(BF16)
    cos_t, sin_t = _rotary_tables()
    q_gain = (sb_q_gain * (SB_HEAD_DIM ** -0.5)).reshape(1, SB_W)
    k_gain = sb_k_gain.reshape(1, SB_W)

    q_sb, k_sb, v_sb, q_r, k_r, v_r = _proj(
        x2d, g_mix, w_in[:, :QKV_W], q_gain, k_gain, cos_t, sin_t, tm=512)

    y_sb = _sb_attention(q_sb.reshape(b, s, SB_W), k_sb.reshape(b, s, SB_W),
                         v_sb.reshape(b, s, SB_W))

    log_gamma = jnp.log1p(-jnp.exp2(-5.0 - jnp.arange(RET_HEADS, dtype=F32)))
    y_r = _retention(log_gamma, q_r.reshape(b, s, RET_QK_W), k_r.reshape(b, s, RET_QK_W),
                     v_r.reshape(b, s, RET_V_W), ret_norm_gain.reshape(1, RET_V_W))

    x1 = _merge(x2d, g_mix, w_in[:, QKV_W:], y_sb.reshape(t, SB_W), y_r.reshape(t, RET_V_W),
                w_branch_sb.astype(BF16), w_branch_ret.astype(BF16), w_out.astype(BF16), tm=512)

    w_up = w_up.astype(BF16)
    out = _ffn_ple(
        x1, p_i.reshape(t, PLE_DIM), g_ffn.reshape(1, d),
        _chunked_cols(w_up[:, :D_FF]), _chunked_cols(w_up[:, D_FF:]),
        _chunked_cols(conv_w), _chunked_cols(conv_b.reshape(1, D_FF)),
        w_down.astype(BF16),
        g_ple.reshape(1, d), w_ple_gate.astype(BF16), w_ple_proj.astype(BF16), tm=512)
    return out.reshape(b, s, d)


def kernel(x, p, g_mix, w_in, sb_q_gain, sb_k_gain, ret_norm_gain, w_branch_sb, w_branch_ret,
           w_out, g_ffn, w_up, conv_w, conv_b, w_down, g_ple, w_ple_gate, w_ple_proj):
    for i in range(p.shape[0]):
        x = _layer(x, p[i], g_mix[i], w_in[i], sb_q_gain[i], sb_k_gain[i], ret_norm_gain[i],
                   w_branch_sb[i], w_branch_ret[i], w_out[i], g_ffn[i], w_up[i], conv_w[i],
                   conv_b[i], w_down[i], g_ple[i], w_ple_gate[i], w_ple_proj[i])
    return x
```

```python
import functools

import jax
import jax.numpy as jnp
from jax import lax
from jax.experimental import pallas as pl
from jax.experimental.pallas import tpu as pltpu

D_MODEL = 1024
BATCH = 16
SEQ = 2048
PLE_DIM = 256
SB_HEADS = 8
SB_HEAD_DIM = 64
RET_HEADS = 8
RET_QK_DIM = 64
RET_V_DIM = 128
RET_CHUNK = 128
ROPE_BASE = 10000.0
D_FF = 2816
CONV_WIDTH = 3
EPS = 1e-6

SB_W = SB_HEADS * SB_HEAD_DIM
RET_QK_W = RET_HEADS * RET_QK_DIM
RET_V_W = RET_HEADS * RET_V_DIM
QKV_W = 3 * SB_W + 2 * RET_QK_W + RET_V_W
GATE_W = RET_V_W + 2 * D_MODEL

LANES = 128
SUBLANES = 8
ATT_BLOCK = 128
ATT_QTILE = 512
ATT_UNROLL = 2
ATT_ROW_PARTS = 2
FF_CHUNK = 256
N_FF_CHUNKS = D_FF // FF_CHUNK
assert N_FF_CHUNKS * FF_CHUNK == D_FF and N_FF_CHUNKS % 2 == 1
VMEM_LIMIT = 56 * 1024 * 1024

F32 = jnp.float32
BF16 = jnp.bfloat16


def _rmsnorm(x, gain):
    ms = jnp.mean(x * x, axis=-1, keepdims=True)
    return x * lax.rsqrt(ms + EPS) * gain


def _sigmoid(x):
    return 1.0 / (1.0 + jnp.exp(-x))


def _split_bf16(x):
    hi = x.astype(BF16)
    lo = (x - hi.astype(F32)).astype(BF16)
    return hi, lo


def _dot(a, b):
    return jnp.dot(a, b, preferred_element_type=F32)


def _dot_nt(a, b):
    return lax.dot_general(a, b, (((1,), (1,)), ((), ())), preferred_element_type=F32)


def _dot_tn(a, b):
    return lax.dot_general(a, b, (((0,), (0,)), ((), ())), preferred_element_type=F32)


def _group_mean_matrix(width, group):
    r = lax.broadcasted_iota(jnp.int32, (width, width), 0) // group
    c = lax.broadcasted_iota(jnp.int32, (width, width), 1) // group
    return jnp.where(r == c, 1.0 / group, 0.0).astype(BF16)


def _head_rmsnorm_cols(y, gain, gmat):
    width = gmat.shape[0]
    parts = []
    for c in range(y.shape[1] // width):
        sq = y[:, c * width:(c + 1) * width]
        sq = sq * sq
        hi, lo = _split_bf16(sq)
        parts.append(_dot(hi, gmat) + _dot(lo, gmat))
    ms = jnp.concatenate(parts, axis=1)
    return y * lax.rsqrt(ms + EPS) * gain


def _rotary_cols(y, cos, sin_signed, first_half):
    width = y.shape[1]
    half = RET_QK_DIM // 2
    up = pltpu.roll(y, width - half, axis=1)
    dn = pltpu.roll(y, half, axis=1)
    partner = jnp.where(first_half, up, dn)
    return y * cos + partner * sin_signed


def _proj_kernel(x_ref, g_ref, w_ref, qg_ref, kg_ref, cos_ref, sin_ref,
                 qsb_ref, ksb_ref, vsb_ref, qr_ref, kr_ref, vr_ref):
    h = _rmsnorm(x_ref[...], g_ref[...]).astype(BF16)
    gmat = _group_mean_matrix(2 * LANES, SB_HEAD_DIM)

    o = 0
    q = _dot(h, w_ref[:, o:o + SB_W]); o += SB_W
    qsb_ref[...] = _head_rmsnorm_cols(q, qg_ref[...], gmat).astype(BF16)
    k = _dot(h, w_ref[:, o:o + SB_W]); o += SB_W
    ksb_ref[...] = _head_rmsnorm_cols(k, kg_ref[...], gmat).astype(BF16)
    vsb_ref[...] = _dot(h, w_ref[:, o:o + SB_W]).astype(BF16); o += SB_W

    reps = RET_QK_W // LANES
    cos = jnp.concatenate([cos_ref[...]] * reps, axis=1)
    sin = jnp.concatenate([sin_ref[...]] * reps, axis=1)
    lane = lax.broadcasted_iota(jnp.int32, (1, RET_QK_W), 1)
    first_half = (lane % RET_QK_DIM) < (RET_QK_DIM // 2)
    q = _dot(h, w_ref[:, o:o + RET_QK_W]); o += RET_QK_W
    qr_ref[...] = _rotary_cols(q, cos, sin, first_half).astype(BF16)
    k = _dot(h, w_ref[:, o:o + RET_QK_W]); o += RET_QK_W
    kr_ref[...] = (_rotary_cols(k, cos, sin, first_half) * (RET_QK_DIM ** -0.5)).astype(BF16)
    vr_ref[...] = _dot(h, w_ref[:, o:o + RET_V_W]).astype(BF16)


def _proj(x2d, g_mix, w_qkv, q_gain, k_gain, cos_t, sin_t, tm):
    t = x2d.shape[0]
    tiles_per_seq = SEQ // tm
    row = lambda i: (i, 0)
    const = lambda i: (0, 0)
    pos = lambda i: (i % tiles_per_seq, 0)
    out_w = (SB_W, SB_W, SB_W, RET_QK_W, RET_QK_W, RET_V_W)
    return pl.pallas_call(
        _proj_kernel,
        grid=(t // tm,),
        in_specs=[
            pl.BlockSpec((tm, D_MODEL), row),
            pl.BlockSpec((1, D_MODEL), const),
            pl.BlockSpec((D_MODEL, QKV_W), const),
            pl.BlockSpec((1, SB_W), const),
            pl.BlockSpec((1, SB_W), const),
            pl.BlockSpec((tm, LANES), pos),
            pl.BlockSpec((tm, LANES), pos),
        ],
        out_specs=[pl.BlockSpec((tm, w), row) for w in out_w],
        out_shape=[jax.ShapeDtypeStruct((t, w), BF16) for w in out_w],
        compiler_params=pltpu.CompilerParams(
            dimension_semantics=("parallel",), vmem_limit_bytes=VMEM_LIMIT),
        name="proj",
    )(x2d, g_mix, w_qkv, q_gain, k_gain, cos_t, sin_t)


def _sb_kernel(q_ref, k_ref, v_ref, o_ref, kst_ref, vst_ref, carry_ref, acc_ref, z_ref, a_ref):
    blk = ATT_BLOCK
    n_blocks = SEQ // blk
    qt = ATT_QTILE
    sub = qt // blk
    lane = lax.broadcasted_iota(jnp.int32, (1, LANES), 1)
    head0 = lane < SB_HEAD_DIM
    zero = jnp.zeros((), BF16)

    for j in range(n_blocks):
        kj = k_ref[0, j * blk:(j + 1) * blk, :]
        vj = v_ref[0, j * blk:(j + 1) * blk, :]
        kst_ref[j, 0:blk, :] = jnp.where(head0, kj, zero)
        kst_ref[j, blk:2 * blk, :] = jnp.where(head0, zero, kj)
        vst_ref[2 * j * blk:(2 * j + 1) * blk, :] = jnp.where(head0, vj, zero)
        vst_ref[(2 * j + 1) * blk:(2 * j + 2) * blk, :] = jnp.where(head0, zero, vj)

    r = lax.broadcasted_iota(jnp.int32, (2 * blk, 2 * blk), 0) % blk
    c = lax.broadcasted_iota(jnp.int32, (2 * blk, 2 * blk), 1)
    cum_mat = jnp.where((r > c) | (c >= blk), 1.0, 0.0).astype(BF16)
    t_idx = lax.broadcasted_iota(jnp.int32, (blk, 2 * blk), 0)
    s_idx = lax.broadcasted_iota(jnp.int32, (blk, 2 * blk), 1) % blk
    causal = s_idx < t_idx

    def mask_first_block(x):
        top = jnp.where(causal, x[0:blk], 0.0)
        return top if x.shape[0] == blk else jnp.concatenate([top, x[blk:]], axis=0)

    def scores(q, j):
        return _dot_nt(q, kst_ref[j])

    def cumulate(z, diag):
        sp = jnp.maximum(z, 0.0) + jnp.log(1.0 + jnp.exp(-jnp.abs(z)))
        if diag:
            sp = mask_first_block(sp)
        hi = sp.astype(BF16)
        s0 = _dot(hi[:, 0:blk], cum_mat[0:blk])
        s1 = _dot(hi[:, blk:], cum_mat[0:blk])
        later = jnp.concatenate([s0[:, 0:blk], s1[:, 0:blk]], axis=1)
        total = jnp.concatenate([s0[:, blk:], s1[:, blk:]], axis=1)
        return z - sp, later, total

    def weights(w, later, carry, diag):
        a = jnp.exp(w - carry - later)
        if diag:
            a = mask_first_block(a)
        return a.astype(BF16)

    def v_rows(j_low, n):
        return pl.ds(pl.multiple_of(j_low * (2 * blk), 2 * blk), n * 2 * blk)

    def diagonal_blocks(q, ti):
        cs = [cumulate(scores(q[jj * blk:], ti * sub + jj), True) for jj in range(sub)]
        for i in range(sub):
            carry = jnp.zeros((blk, 2 * blk), F32)
            a_parts = [None] * (i + 1)
            for jj in reversed(range(i + 1)):
                w, later, total = cs[jj]
                rows = slice((i - jj) * blk, (i - jj + 1) * blk)
                a = jnp.exp(w[rows] - carry - later[rows])
                if jj == i:
                    a = jnp.where(causal, a, 0.0)
                a_parts[jj] = a.astype(BF16)
                carry = carry + total[rows]
            carry_ref[i * blk:(i + 1) * blk, :] = carry
            acc_ref[i * blk:(i + 1) * blk, :] = _dot(
                jnp.concatenate(a_parts, axis=1), vst_ref[v_rows(ti * sub, i + 1), :])

    part = qt // ATT_ROW_PARTS
    row_parts = [(p * part, (p + 1) * part) for p in range(ATT_ROW_PARTS)]
    n_keys = ATT_UNROLL
    assert sub == 2 * n_keys

    def issue_scores(q, j_low, slot):
        for p, (r0, r1) in enumerate(row_parts):
            for u in range(n_keys):
                z_ref[slot, p, u] = scores(q[r0:r1], j_low + u)

    def flush_values(slot, j_low):
        for p, (r0, r1) in enumerate(row_parts):
            acc_ref[r0:r1, :] += _dot(a_ref[slot, p], vst_ref[v_rows(j_low, n_keys), :])

    def step(q, slot, j_low, j_low_next, j_low_prev):
        flush_values(1 - slot, j_low_prev)
        issue_scores(q, j_low_next, 1 - slot)
        cs = [[cumulate(z_ref[slot, p, u], False) for u in reversed(range(n_keys))]
              for p in range(len(row_parts))]
        for p, (r0, r1) in enumerate(row_parts):
            carry = carry_ref[r0:r1, :]
            a_parts = []
            for w, later, total in cs[p]:
                a_parts.append(weights(w, later, carry, False))
                carry = carry + total
            carry_ref[r0:r1, :] = carry
            a_ref[slot, p] = jnp.concatenate(a_parts[::-1], axis=1)

    def query_tile(ti, _):
        rows = pl.ds(pl.multiple_of(ti * qt, qt), qt)
        q = q_ref[0, rows, :]
        issue_scores(q, jnp.maximum(ti * sub - n_keys, 0), 0)
        a_ref[1] = jnp.zeros_like(a_ref[1])
        diagonal_blocks(q, ti)

        def earlier(n, _):
            base = (ti - 1 - n) * sub
            step(q, 0, base + n_keys, base, base + 2 * n_keys)
            step(q, 1, base, jnp.maximum(base - n_keys, 0), base + n_keys)
            return 0

        lax.fori_loop(0, ti, earlier, 0)
        flush_values(1, 0)
        o_ref[0, rows, :] = acc_ref[...].astype(BF16)
        return 0

    lax.fori_loop(0, SEQ // qt, query_tile, 0)


def _sb_attention(q, k, v):
    b = q.shape[0]
    pairs = SB_W // LANES
    spec = pl.BlockSpec((1, SEQ, LANES), lambda bi, hp: (bi, 0, hp))
    stacked = pltpu.VMEM((SEQ // ATT_BLOCK, 2 * ATT_BLOCK, LANES), BF16)
    part = ATT_QTILE // ATT_ROW_PARTS
    return pl.pallas_call(
        _sb_kernel,
        grid=(b, pairs),
        in_specs=[spec, spec, spec],
        out_specs=spec,
        out_shape=jax.ShapeDtypeStruct((b, SEQ, SB_W), BF16),
        scratch_shapes=[
            stacked,
            pltpu.VMEM((2 * SEQ, LANES), BF16),
            pltpu.VMEM((ATT_QTILE, 2 * ATT_BLOCK), F32),
            pltpu.VMEM((ATT_QTILE, LANES), F32),
            pltpu.VMEM((2, ATT_ROW_PARTS, ATT_UNROLL, part, 2 * ATT_BLOCK), F32),
            pltpu.VMEM((2, ATT_ROW_PARTS, part, ATT_UNROLL * 2 * ATT_BLOCK), BF16),
        ],
        compiler_params=pltpu.CompilerParams(
            dimension_semantics=("parallel", "parallel"), vmem_limit_bytes=VMEM_LIMIT),
        name="sb_attn",
    )(q, k, v)


def _ret_kernel(lg_ref, q_ref, k_ref, v_ref, gain_ref, o_ref,
                kst_ref, vbd_ref, inner_ref, update_ref, state_ref):
    c = RET_CHUNK
    dv = RET_V_DIM
    n_chunks = SEQ // c
    hp = pl.program_id(1)
    zero = jnp.zeros((), BF16)
    qk_head0 = lax.broadcasted_iota(jnp.int32, (1, LANES), 1) < RET_QK_DIM
    col_head0 = lax.broadcasted_iota(jnp.int32, (1, 2 * dv), 1) < dv
    lg0 = lg_ref[2 * hp]
    lg1 = lg_ref[2 * hp + 1]
    lg_col = jnp.where(col_head0, lg0, lg1)
    lg_qk = jnp.where(qk_head0, lg0, lg1)

    row = lax.broadcasted_iota(jnp.int32, (c, 2 * c), 0).astype(F32)
    col = (lax.broadcasted_iota(jnp.int32, (c, 2 * c), 1) % c).astype(F32)
    diff = row - col
    d_local = jnp.where(diff >= 0.0, jnp.exp(jnp.maximum(diff, 0.0) * lg_col), 0.0)
    xi = jnp.exp((row + 1.0) * lg_col)
    g_chunk = jnp.exp(jnp.full((1, 2 * dv), c, F32) * lg_col)
    row_k = lax.broadcasted_iota(jnp.int32, (c, LANES), 0).astype(F32)
    zeta = jnp.exp((c - 1.0 - row_k) * lg_qk)
    own_block = ((lax.broadcasted_iota(jnp.int32, (LANES, 2 * dv), 0) < RET_QK_DIM)
                 == (lax.broadcasted_iota(jnp.int32, (LANES, 2 * dv), 1) < dv))

    for n in range(n_chunks):
        rows = slice(n * c, (n + 1) * c)
        kn = k_ref[0, rows, :]
        vn = v_ref[0, rows, :]
        kst_ref[n, 0:c, :] = jnp.where(qk_head0, kn, zero)
        kst_ref[n, c:2 * c, :] = jnp.where(qk_head0, zero, kn)
        vbd_ref[n, 0:c, :] = jnp.where(col_head0, vn, zero)
        vbd_ref[n, c:2 * c, :] = jnp.where(col_head0, zero, vn)

    for n in range(n_chunks):
        rows = slice(n * c, (n + 1) * c)
        inner = _dot_nt(q_ref[0, rows, :], kst_ref[n]) * d_local
        inner_ref[n] = inner.astype(BF16)
        kz = (k_ref[0, rows, :].astype(F32) * zeta).astype(BF16)
        update = _dot_tn(kz, v_ref[0, rows, :])
        update_ref[n] = jnp.where(own_block, update, 0.0)

    state = jnp.zeros((LANES, 2 * dv), F32)
    for n in range(n_chunks):
        state_ref[n] = state.astype(BF16)
        state = state * g_chunk + update_ref[n]

    for n in range(n_chunks):
        rows = slice(n * c, (n + 1) * c)
        qc = q_ref[0, rows, :]
        y = _dot(inner_ref[n], vbd_ref[n]) + _dot(qc, state_ref[n]) * xi
        for hh in range(2):
            yh = y[:, hh * dv:(hh + 1) * dv]
            ms = jnp.mean(yh * yh, axis=-1, keepdims=True)
            yh = yh * lax.rsqrt(ms + EPS) * gain_ref[:, hh * dv:(hh + 1) * dv]
            o_ref[0, rows, hh * dv:(hh + 1) * dv] = yh.astype(o_ref.dtype)


def _retention(log_gamma, q, k, v, gain):
    b = q.shape[0]
    pairs = RET_QK_W // LANES
    qk_spec = pl.BlockSpec((1, SEQ, LANES), lambda bi, hp, lg: (bi, 0, hp))
    v_spec = pl.BlockSpec((1, SEQ, 2 * RET_V_DIM), lambda bi, hp, lg: (bi, 0, hp))
    return pl.pallas_call(
        _ret_kernel,
        grid_spec=pltpu.PrefetchScalarGridSpec(
            num_scalar_prefetch=1,
            grid=(b, pairs),
            in_specs=[qk_spec, qk_spec, v_spec,
                      pl.BlockSpec((1, 2 * RET_V_DIM), lambda bi, hp, lg: (0, hp))],
            out_specs=v_spec,
            scratch_shapes=[
                pltpu.VMEM((SEQ // RET_CHUNK, 2 * RET_CHUNK, LANES), BF16),
                pltpu.VMEM((SEQ // RET_CHUNK, 2 * RET_CHUNK, 2 * RET_V_DIM), BF16),
                pltpu.VMEM((SEQ // RET_CHUNK, RET_CHUNK, 2 * RET_CHUNK), BF16),
                pltpu.VMEM((SEQ // RET_CHUNK, LANES, 2 * RET_V_DIM), F32),
                pltpu.VMEM((SEQ // RET_CHUNK, LANES, 2 * RET_V_DIM), BF16),
            ],
        ),
        out_shape=jax.ShapeDtypeStruct((b, SEQ, RET_V_W), BF16),
        compiler_params=pltpu.CompilerParams(
            dimension_semantics=("parallel", "parallel"), vmem_limit_bytes=VMEM_LIMIT),
        name="ret",
    )(log_gamma, q, k, v, gain)


def _merge_kernel(x_ref, g_ref, wg_ref, ysb_ref, yr_ref, wsb_ref, wret_ref, wo_ref, o_ref):
    x = x_ref[...]
    h = _rmsnorm(x, g_ref[...]).astype(BF16)
    g_r = _dot(h, wg_ref[:, 0:RET_V_W])
    y_r = (yr_ref[...].astype(F32) * (g_r * _sigmoid(g_r))).astype(BF16)
    gate_sb = _dot(h, wg_ref[:, RET_V_W:RET_V_W + D_MODEL])
    merged = _sigmoid(gate_sb) * _dot(ysb_ref[...], wsb_ref[...])
    gate_ret = _dot(h, wg_ref[:, RET_V_W + D_MODEL:GATE_W])
    merged = merged + _sigmoid(gate_ret) * _dot(y_r, wret_ref[...])
    o_ref[...] = x + _dot(merged.astype(BF16), wo_ref[...])


def _merge(x2d, g_mix, w_gate, y_sb, y_r, w_sb, w_ret, w_out, tm):
    t = x2d.shape[0]
    row = lambda i: (i, 0)
    const = lambda i: (0, 0)
    return pl.pallas_call(
        _merge_kernel,
        grid=(t // tm,),
        in_specs=[
            pl.BlockSpec((tm, D_MODEL), row),
            pl.BlockSpec((1, D_MODEL), const),
            pl.BlockSpec((D_MODEL, GATE_W), const),
            pl.BlockSpec((tm, SB_W), row),
            pl.BlockSpec((tm, RET_V_W), row),
            pl.BlockSpec((SB_W, D_MODEL), const),
            pl.BlockSpec((RET_V_W, D_MODEL), const),
            pl.BlockSpec((D_MODEL, D_MODEL), const),
        ],
        out_specs=pl.BlockSpec((tm, D_MODEL), row),
        out_shape=jax.ShapeDtypeStruct((t, D_MODEL), F32),
        compiler_params=pltpu.CompilerParams(
            dimension_semantics=("parallel",), vmem_limit_bytes=VMEM_LIMIT),
        name="merge",
    )(x2d, g_mix, w_gate, y_sb, y_r, w_sb, w_ret, w_out)


def _gelu_tanh(x):
    return 0.5 * x * (1.0 + jnp.tanh(0.7978845608028654 * (x + 0.044715 * (x * x * x))))


def _ffn_kernel(tiles_per_seq, x_ref, p_ref, gf_ref, wup_ref, cw_ref, cb_ref, wd_ref,
                gp_ref, wpg_ref, wpp_ref, o_ref, halo_ref, h_ref, au_ref, g_ref):
    tm = x_ref.shape[0]
    h_ref[...] = _rmsnorm(x_ref[...], gf_ref[...]).astype(BF16)
    row8 = lax.broadcasted_iota(jnp.int32, (SUBLANES, FF_CHUNK), 0)

    @pl.when(pl.program_id(0) % tiles_per_seq == 0)
    def _():
        halo_ref[...] = jnp.zeros_like(halo_ref)

    def chunk_cols(ci, offset=0):
        return pl.ds(pl.multiple_of(offset + ci * FF_CHUNK, FF_CHUNK), FF_CHUNK)

    def up(ci, slot):
        h = h_ref[...]
        au_ref[slot, 0] = _dot(h, wup_ref[:, chunk_cols(ci)])
        au_ref[slot, 1] = _dot(h, wup_ref[:, chunk_cols(ci, D_FF)])

    def gated(ci, slot):
        a = au_ref[slot, 0]
        u = au_ref[slot, 1]
        prev = halo_ref[ci]
        halo_ref[ci] = a[tm - SUBLANES:tm]
        a1 = pltpu.roll(a, 1, axis=0)
        a2 = pltpu.roll(a, 2, axis=0)
        p1 = pltpu.roll(prev, 1, axis=0)
        p2 = pltpu.roll(prev, 2, axis=0)
        a1 = jnp.concatenate([jnp.where(row8 < 1, p1, a1[0:SUBLANES]), a1[SUBLANES:]], axis=0)
        a2 = jnp.concatenate([jnp.where(row8 < 2, p2, a2[0:SUBLANES]), a2[SUBLANES:]], axis=0)
        cw = cw_ref[:, chunk_cols(ci)]
        conv = cb_ref[:, chunk_cols(ci)] + cw[0:1] * a2 + cw[1:2] * a1 + cw[2:3] * a
        g_ref[:, chunk_cols(ci)] = (_gelu_tanh(conv) * u).astype(BF16)

    up(0, 0)

    def chunk_pair(k, _):
        up(2 * k + 1, 1)
        gated(2 * k, 0)
        up(2 * k + 2, 0)
        gated(2 * k + 1, 1)
        return 0

    lax.fori_loop(0, (N_FF_CHUNKS - 1) // 2, chunk_pair, 0)

    last = N_FF_CHUNKS - 1
    done = last * FF_CHUNK
    down = _dot(g_ref[:, 0:done], wd_ref[0:done, :])
    gated(last, last % 2)
    x2 = x_ref[...] + (down + _dot(g_ref[:, done:D_FF], wd_ref[done:D_FF, :]))

    emb = _dot(p_ref[...].astype(BF16), wpp_ref[...])
    h3 = _rmsnorm(x2, gp_ref[...]).astype(BF16)
    gate = _sigmoid(_dot(h3, wpg_ref[...]))
    o_ref[...] = x2 + gate * emb


def _ffn_ple(x2d, p2d, g_ffn, w_up, cw, cb, wd, g_ple, w_pg, w_pp, tm):
    t = x2d.shape[0]
    row = lambda i: (i, 0)
    c2 = lambda i: (0, 0)
    once = pl.Buffered(1)
    return pl.pallas_call(
        functools.partial(_ffn_kernel, SEQ // tm),
        grid=(t // tm,),
        in_specs=[
            pl.BlockSpec((tm, D_MODEL), row),
            pl.BlockSpec((tm, PLE_DIM), row),
            pl.BlockSpec((1, D_MODEL), c2),
            pl.BlockSpec((D_MODEL, 2 * D_FF), c2, pipeline_mode=once),
            pl.BlockSpec((CONV_WIDTH, D_FF), c2),
            pl.BlockSpec((1, D_FF), c2),
            pl.BlockSpec((D_FF, D_MODEL), c2, pipeline_mode=once),
            pl.BlockSpec((1, D_MODEL), c2),
            pl.BlockSpec((D_MODEL, D_MODEL), c2, pipeline_mode=once),
            pl.BlockSpec((PLE_DIM, D_MODEL), c2, pipeline_mode=once),
        ],
        out_specs=pl.BlockSpec((tm, D_MODEL), row),
        out_shape=jax.ShapeDtypeStruct((t, D_MODEL), F32),
        scratch_shapes=[
            pltpu.VMEM((N_FF_CHUNKS, SUBLANES, FF_CHUNK), F32),
            pltpu.VMEM((tm, D_MODEL), BF16),
            pltpu.VMEM((2, 2, tm, FF_CHUNK), F32),
            pltpu.VMEM((tm, D_FF), BF16),
        ],
        compiler_params=pltpu.CompilerParams(
            dimension_semantics=("arbitrary",), vmem_limit_bytes=VMEM_LIMIT),
        name="ffn_ple",
    )(x2d, p2d, g_ffn, w_up, cw, cb, wd, g_ple, w_pg, w_pp)


def _rotary_tables():
    half = RET_QK_DIM // 2
    inv = ROPE_BASE ** (-jnp.arange(half, dtype=F32) * 2.0 / RET_QK_DIM)
    ang = jnp.arange(SEQ, dtype=F32)[:, None] * inv[None, :]
    reps = LANES // half
    cos = jnp.tile(jnp.cos(ang), (1, reps))
    sin = jnp.tile(jnp.sin(ang), (1, reps))
    first_half = (jnp.arange(LANES) % RET_QK_DIM) < half
    return cos, jnp.where(first_half[None, :], -sin, sin)


def _layer(x, p_i, g_mix, w_in, sb_q_gain, sb_k_gain, ret_norm_gain, w_branch_sb,
           w_branch_ret, w_out, g_ffn, w_up, conv_w, conv_b, w_down, g_ple,
           w_ple_gate, w_ple_proj):
    b, s, d = x.shape
    t = b * s
    x2d = x.reshape(t, d)
    g_mix = g_mix.reshape(1, d)
    w_in = w_in.astype(BF16)
    cos_t, sin_t = _rotary_tables()
    q_gain = (sb_q_gain * (SB_HEAD_DIM ** -0.5)).reshape(1, SB_W)
    k_gain = sb_k_gain.reshape(1, SB_W)

    q_sb, k_sb, v_sb, q_r, k_r, v_r = _proj(
        x2d, g_mix, w_in[:, :QKV_W], q_gain, k_gain, cos_t, sin_t, tm=512)

    y_sb = _sb_attention(q_sb.reshape(b, s, SB_W), k_sb.reshape(b, s, SB_W),
                         v_sb.reshape(b, s, SB_W))

    log_gamma = jnp.log1p(-jnp.exp2(-5.0 - jnp.arange(RET_HEADS, dtype=F32)))
    y_r = _retention(log_gamma, q_r.reshape(b, s, RET_QK_W), k_r.reshape(b, s, RET_QK_W),
                     v_r.reshape(b, s, RET_V_W), ret_norm_gain.reshape(1, RET_V_W))

    x1 = _merge(x2d, g_mix, w_in[:, QKV_W:], y_sb.reshape(t, SB_W), y_r.reshape(t, RET_V_W),
                w_branch_sb.astype(BF16), w_branch_ret.astype(BF16), w_out.astype(BF16), tm=512)

    out = _ffn_ple(
        x1, p_i.reshape(t, PLE_DIM), g_ffn.reshape(1, d),
        w_up.astype(BF16), conv_w, conv_b.reshape(1, D_FF), w_down.astype(BF16),
        g_ple.reshape(1, d), w_ple_gate.astype(BF16), w_ple_proj.astype(BF16), tm=512)
    return out.reshape(b, s, d)


def kernel(x, p, g_mix, w_in, sb_q_gain, sb_k_gain, ret_norm_gain, w_branch_sb, w_branch_ret,
           w_out, g_ffn, w_up, conv_w, conv_b, w_down, g_ple, w_ple_gate, w_ple_proj):
    for i in range(p.shape[0]):
        x = _layer(x, p[i], g_mix[i], w_in[i], sb_q_gain[i], sb_k_gain[i], ret_norm_gain[i],
                   w_branch_sb[i], w_branch_ret[i], w_out[i], g_ffn[i], w_up[i], conv_w[i],
                   conv_b[i], w_down[i], g_ple[i], w_ple_gate[i], w_ple_proj[i])
    return x
```

```python
import functools

import jax
import jax.numpy as jnp
from jax import lax
from jax.experimental import pallas as pl
from jax.experimental.pallas import tpu as pltpu

D_MODEL = 1024
BATCH = 16
SEQ = 2048
PLE_DIM = 256
SB_HEADS = 8
SB_HEAD_DIM = 64
RET_HEADS = 8
RET_QK_DIM = 64
RET_V_DIM = 128
RET_CHUNK = 128
ROPE_BASE = 10000.0
D_FF = 2816
CONV_WIDTH = 3
EPS = 1e-6

SB_W = SB_HEADS * SB_HEAD_DIM
RET_QK_W = RET_HEADS * RET_QK_DIM
RET_V_W = RET_HEADS * RET_V_DIM
QKV_W = 3 * SB_W + 2 * RET_QK_W + RET_V_W
GATE_W = RET_V_W + 2 * D_MODEL

LANES = 128
SUBLANES = 8
ATT_BLOCK = 128
ATT_QTILE = 512
ATT_UNROLL = 2
ATT_ROW_PARTS = 2
LOG2E = 1.4426950408889634
FF_CHUNK = 256
N_FF_CHUNKS = D_FF // FF_CHUNK
assert N_FF_CHUNKS * FF_CHUNK == D_FF and N_FF_CHUNKS % 2 == 1
VMEM_LIMIT = 56 * 1024 * 1024

F32 = jnp.float32
BF16 = jnp.bfloat16


def _rmsnorm(x, gain):
    ms = jnp.mean(x * x, axis=-1, keepdims=True)
    return x * lax.rsqrt(ms + EPS) * gain


def _sigmoid(x):
    return 1.0 / (1.0 + jnp.exp(-x))


def _dot(a, b):
    return jnp.dot(a, b, preferred_element_type=F32)


def _dot_nt(a, b):
    return lax.dot_general(a, b, (((1,), (1,)), ((), ())), preferred_element_type=F32)


def _dot_tn(a, b):
    return lax.dot_general(a, b, (((0,), (0,)), ((), ())), preferred_element_type=F32)


def _group_mean_matrix(width, group):
    r = lax.broadcasted_iota(jnp.int32, (width, width), 0) // group
    c = lax.broadcasted_iota(jnp.int32, (width, width), 1) // group
    return jnp.where(r == c, 1.0 / group, 0.0).astype(BF16)


def _head_rmsnorm_cols(y, gain, gmat):
    width = gmat.shape[0]
    parts = []
    for c in range(y.shape[1] // width):
        sq = y[:, c * width:(c + 1) * width]
        parts.append(_dot((sq * sq).astype(BF16), gmat))
    ms = jnp.concatenate(parts, axis=1)
    return y * lax.rsqrt(ms + EPS) * gain


def _rotary_cols(y, cos, sin_signed, first_half):
    width = y.shape[1]
    half = RET_QK_DIM // 2
    up = pltpu.roll(y, width - half, axis=1)
    dn = pltpu.roll(y, half, axis=1)
    partner = jnp.where(first_half, up, dn)
    return y * cos + partner * sin_signed


def _proj_kernel(x_ref, g_ref, w_ref, qg_ref, kg_ref, cos_ref, sin_ref,
                 qsb_ref, ksb_ref, vsb_ref, qr_ref, kr_ref, vr_ref):
    h = _rmsnorm(x_ref[...], g_ref[...]).astype(BF16)
    gmat = _group_mean_matrix(2 * LANES, SB_HEAD_DIM)

    o = 0
    q = _dot(h, w_ref[:, o:o + SB_W]); o += SB_W
    qsb_ref[...] = _head_rmsnorm_cols(q, qg_ref[...], gmat).astype(BF16)
    k = _dot(h, w_ref[:, o:o + SB_W]); o += SB_W
    ksb_ref[...] = _head_rmsnorm_cols(k, kg_ref[...], gmat).astype(BF16)
    vsb_ref[...] = _dot(h, w_ref[:, o:o + SB_W]).astype(BF16); o += SB_W

    reps = RET_QK_W // LANES
    cos = jnp.concatenate([cos_ref[...]] * reps, axis=1)
    sin = jnp.concatenate([sin_ref[...]] * reps, axis=1)
    lane = lax.broadcasted_iota(jnp.int32, (1, RET_QK_W), 1)
    first_half = (lane % RET_QK_DIM) < (RET_QK_DIM // 2)
    q = _dot(h, w_ref[:, o:o + RET_QK_W]); o += RET_QK_W
    qr_ref[...] = _rotary_cols(q, cos, sin, first_half).astype(BF16)
    k = _dot(h, w_ref[:, o:o + RET_QK_W]); o += RET_QK_W
    kr_ref[...] = (_rotary_cols(k, cos, sin, first_half) * (RET_QK_DIM ** -0.5)).astype(BF16)
    vr_ref[...] = _dot(h, w_ref[:, o:o + RET_V_W]).astype(BF16)


def _proj(x2d, g_mix, w_qkv, q_gain, k_gain, cos_t, sin_t, tm):
    t = x2d.shape[0]
    tiles_per_seq = SEQ // tm
    row = lambda i: (i, 0)
    const = lambda i: (0, 0)
    pos = lambda i: (i % tiles_per_seq, 0)
    out_w = (SB_W, SB_W, SB_W, RET_QK_W, RET_QK_W, RET_V_W)
    return pl.pallas_call(
        _proj_kernel,
        grid=(t // tm,),
        in_specs=[
            pl.BlockSpec((tm, D_MODEL), row),
            pl.BlockSpec((1, D_MODEL), const),
            pl.BlockSpec((D_MODEL, QKV_W), const),
            pl.BlockSpec((1, SB_W), const),
            pl.BlockSpec((1, SB_W), const),
            pl.BlockSpec((tm, LANES), pos),
            pl.BlockSpec((tm, LANES), pos),
        ],
        out_specs=[pl.BlockSpec((tm, w), row) for w in out_w],
        out_shape=[jax.ShapeDtypeStruct((t, w), BF16) for w in out_w],
        compiler_params=pltpu.CompilerParams(
            dimension_semantics=("parallel",), vmem_limit_bytes=VMEM_LIMIT),
        name="proj",
    )(x2d, g_mix, w_qkv, q_gain, k_gain, cos_t, sin_t)


def _sb_kernel(q_ref, k_ref, v_ref, o_ref, kst_ref, vst_ref, carry_ref, acc_ref, z_ref, a_ref):
    blk = ATT_BLOCK
    n_blocks = SEQ // blk
    qt = ATT_QTILE
    sub = qt // blk
    lane = lax.broadcasted_iota(jnp.int32, (1, LANES), 1)
    head0 = lane < SB_HEAD_DIM
    zero = jnp.zeros((), BF16)

    for j in range(n_blocks):
        kj = k_ref[0, j * blk:(j + 1) * blk, :]
        vj = v_ref[0, j * blk:(j + 1) * blk, :]
        kst_ref[j, 0:blk, :] = jnp.where(head0, kj, zero)
        kst_ref[j, blk:2 * blk, :] = jnp.where(head0, zero, kj)
        vst_ref[2 * j * blk:(2 * j + 1) * blk, :] = jnp.where(head0, vj, zero)
        vst_ref[(2 * j + 1) * blk:(2 * j + 2) * blk, :] = jnp.where(head0, zero, vj)

    r = lax.broadcasted_iota(jnp.int32, (blk, 2 * blk), 0)
    c = lax.broadcasted_iota(jnp.int32, (blk, 2 * blk), 1)
    cum_mat = jnp.where((r > c) | (c >= blk), 1.0, 0.0).astype(BF16)
    t_idx = lax.broadcasted_iota(jnp.int32, (blk, 2 * blk), 0)
    s_idx = lax.broadcasted_iota(jnp.int32, (blk, 2 * blk), 1) % blk
    causal = s_idx < t_idx

    def mask_first_block(x):
        top = jnp.where(causal, x[0:blk], 0.0)
        return top if x.shape[0] == blk else jnp.concatenate([top, x[blk:]], axis=0)

    def scores(q, j):
        return _dot_nt(q, kst_ref[j])

    def cumulate(z, diag):
        sp = jnp.maximum(z, 0.0) + jnp.log(1.0 + jnp.exp2(-jnp.abs(z))) * LOG2E
        if diag:
            sp = mask_first_block(sp)
        sp16 = sp.astype(BF16)
        s0 = _dot(sp16[:, 0:blk], cum_mat)
        s1 = _dot(sp16[:, blk:], cum_mat)
        later = jnp.concatenate([s0[:, 0:blk], s1[:, 0:blk]], axis=1)
        total = jnp.concatenate([s0[:, blk:], s1[:, blk:]], axis=1)
        return z - sp, later, total

    def weights(w, later, carry):
        return jnp.exp2(w - carry - later).astype(BF16)

    def v_rows(j_low, n):
        start = j_low * (2 * blk)
        if not isinstance(start, int):
            start = pl.multiple_of(start, 2 * blk)
        return pl.ds(start, n * 2 * blk)

    def diagonal_blocks(q, ti):
        cs = [cumulate(scores(q[jj * blk:], ti * sub + jj), True) for jj in range(sub)]
        for i in range(sub):
            carry = jnp.zeros((blk, 2 * blk), F32)
            a_parts = [None] * (i + 1)
            for jj in reversed(range(i + 1)):
                w, later, total = cs[jj]
                rows = slice((i - jj) * blk, (i - jj + 1) * blk)
                a = jnp.exp2(w[rows] - carry - later[rows])
                if jj == i:
                    a = jnp.where(causal, a, 0.0)
                a_parts[jj] = a.astype(BF16)
                carry = carry + total[rows]
            carry_ref[i * blk:(i + 1) * blk, :] = carry
            acc_ref[i * blk:(i + 1) * blk, :] = _dot(
                jnp.concatenate(a_parts, axis=1), vst_ref[v_rows(ti * sub, i + 1), :])

    part = qt // ATT_ROW_PARTS
    row_parts = [(p * part, (p + 1) * part) for p in range(ATT_ROW_PARTS)]
    n_keys = ATT_UNROLL
    assert sub == 2 * n_keys

    def issue_scores(q, j_low, slot):
        for p, (r0, r1) in enumerate(row_parts):
            for u in range(n_keys):
                z_ref[slot, p, u] = scores(q[r0:r1], j_low + u)

    def flush_values(slot, j_low):
        for p, (r0, r1) in enumerate(row_parts):
            acc_ref[r0:r1, :] += _dot(a_ref[slot, p], vst_ref[v_rows(j_low, n_keys), :])

    def step(q, slot, j_low, j_low_next, j_low_prev):
        flush_values(1 - slot, j_low_prev)
        issue_scores(q, j_low_next, 1 - slot)
        cs = [[cumulate(z_ref[slot, p, u], False) for u in reversed(range(n_keys))]
              for p in range(len(row_parts))]
        for p, (r0, r1) in enumerate(row_parts):
            carry = carry_ref[r0:r1, :]
            a_parts = []
            for w, later, total in cs[p]:
                a_parts.append(weights(w, later, carry))
                carry = carry + total
            carry_ref[r0:r1, :] = carry
            a_ref[slot, p] = jnp.concatenate(a_parts[::-1], axis=1)

    def query_tile(ti):
        rows = slice(ti * qt, (ti + 1) * qt)
        q = q_ref[0, rows, :]
        if ti > 0:
            issue_scores(q, ti * sub - n_keys, 0)
            a_ref[1] = jnp.zeros_like(a_ref[1])
        diagonal_blocks(q, ti)
        if ti > 0:
            def earlier(n, _):
                base = (ti - 1 - n) * sub
                step(q, 0, base + n_keys, base, base + 2 * n_keys)
                step(q, 1, base, jnp.maximum(base - n_keys, 0), base + n_keys)
                return 0

            lax.fori_loop(0, ti, earlier, 0)
            flush_values(1, 0)
        o_ref[0, rows, :] = acc_ref[...].astype(BF16)

    for ti in range(SEQ // qt):
        query_tile(ti)


def _sb_attention(q, k, v):
    b = q.shape[0]
    pairs = SB_W // LANES
    spec = pl.BlockSpec((1, SEQ, LANES), lambda bi, hp: (bi, 0, hp))
    stacked = pltpu.VMEM((SEQ // ATT_BLOCK, 2 * ATT_BLOCK, LANES), BF16)
    part = ATT_QTILE // ATT_ROW_PARTS
    return pl.pallas_call(
        _sb_kernel,
        grid=(b, pairs),
        in_specs=[spec, spec, spec],
        out_specs=spec,
        out_shape=jax.ShapeDtypeStruct((b, SEQ, SB_W), BF16),
        scratch_shapes=[
            stacked,
            pltpu.VMEM((2 * SEQ, LANES), BF16),
            pltpu.VMEM((ATT_QTILE, 2 * ATT_BLOCK), F32),
            pltpu.VMEM((ATT_QTILE, LANES), F32),
            pltpu.VMEM((2, ATT_ROW_PARTS, ATT_UNROLL, part, 2 * ATT_BLOCK), F32),
            pltpu.VMEM((2, ATT_ROW_PARTS, part, ATT_UNROLL * 2 * ATT_BLOCK), BF16),
        ],
        compiler_params=pltpu.CompilerParams(
            dimension_semantics=("parallel", "parallel"), vmem_limit_bytes=VMEM_LIMIT),
        name="sb_attn",
    )(q, k, v)


def _ret_kernel(lg_ref, q_ref, k_ref, v_ref, gain_ref, o_ref,
                kst_ref, vbd_ref, inner_ref, update_ref, state_ref):
    c = RET_CHUNK
    dv = RET_V_DIM
    n_chunks = SEQ // c
    hp = pl.program_id(1)
    zero = jnp.zeros((), BF16)
    qk_head0 = lax.broadcasted_iota(jnp.int32, (1, LANES), 1) < RET_QK_DIM
    col_head0 = lax.broadcasted_iota(jnp.int32, (1, 2 * dv), 1) < dv
    lg0 = lg_ref[2 * hp]
    lg1 = lg_ref[2 * hp + 1]
    lg_col = jnp.where(col_head0, lg0, lg1)
    lg_qk = jnp.where(qk_head0, lg0, lg1)

    row = lax.broadcasted_iota(jnp.int32, (c, 2 * c), 0).astype(F32)
    col = (lax.broadcasted_iota(jnp.int32, (c, 2 * c), 1) % c).astype(F32)
    diff = row - col
    d_local = jnp.where(diff >= 0.0, jnp.exp(jnp.maximum(diff, 0.0) * lg_col), 0.0)
    xi = jnp.exp((row + 1.0) * lg_col)
    g_chunk = jnp.exp(jnp.full((1, 2 * dv), c, F32) * lg_col)
    row_k = lax.broadcasted_iota(jnp.int32, (c, LANES), 0).astype(F32)
    zeta = jnp.exp((c - 1.0 - row_k) * lg_qk)
    own_block = ((lax.broadcasted_iota(jnp.int32, (LANES, 2 * dv), 0) < RET_QK_DIM)
                 == (lax.broadcasted_iota(jnp.int32, (LANES, 2 * dv), 1) < dv))

    for n in range(n_chunks):
        rows = slice(n * c, (n + 1) * c)
        kn = k_ref[0, rows, :]
        vn = v_ref[0, rows, :]
        kst_ref[n, 0:c, :] = jnp.where(qk_head0, kn, zero)
        kst_ref[n, c:2 * c, :] = jnp.where(qk_head0, zero, kn)
        vbd_ref[n, 0:c, :] = jnp.where(col_head0, vn, zero)
        vbd_ref[n, c:2 * c, :] = jnp.where(col_head0, zero, vn)

    for n in range(n_chunks):
        rows = slice(n * c, (n + 1) * c)
        inner = _dot_nt(q_ref[0, rows, :], kst_ref[n]) * d_local
        inner_ref[n] = inner.astype(BF16)
        kz = (k_ref[0, rows, :].astype(F32) * zeta).astype(BF16)
        update = _dot_tn(kz, v_ref[0, rows, :])
        update_ref[n] = jnp.where(own_block, update, 0.0)

    state = jnp.zeros((LANES, 2 * dv), F32)
    for n in range(n_chunks):
        state_ref[n] = state.astype(BF16)
        state = state * g_chunk + update_ref[n]

    for n in range(n_chunks):
        rows = slice(n * c, (n + 1) * c)
        qc = q_ref[0, rows, :]
        y = _dot(inner_ref[n], vbd_ref[n]) + _dot(qc, state_ref[n]) * xi
        for hh in range(2):
            yh = y[:, hh * dv:(hh + 1) * dv]
            ms = jnp.mean(yh * yh, axis=-1, keepdims=True)
            yh = yh * lax.rsqrt(ms + EPS) * gain_ref[:, hh * dv:(hh + 1) * dv]
            o_ref[0, rows, hh * dv:(hh + 1) * dv] = yh.astype(o_ref.dtype)


def _retention(log_gamma, q, k, v, gain):
    b = q.shape[0]
    pairs = RET_QK_W // LANES
    qk_spec = pl.BlockSpec((1, SEQ, LANES), lambda bi, hp, lg: (bi, 0, hp))
    v_spec = pl.BlockSpec((1, SEQ, 2 * RET_V_DIM), lambda bi, hp, lg: (bi, 0, hp))
    return pl.pallas_call(
        _ret_kernel,
        grid_spec=pltpu.PrefetchScalarGridSpec(
            num_scalar_prefetch=1,
            grid=(b, pairs),
            in_specs=[qk_spec, qk_spec, v_spec,
                      pl.BlockSpec((1, 2 * RET_V_DIM), lambda bi, hp, lg: (0, hp))],
            out_specs=v_spec,
            scratch_shapes=[
                pltpu.VMEM((SEQ // RET_CHUNK, 2 * RET_CHUNK, LANES), BF16),
                pltpu.VMEM((SEQ // RET_CHUNK, 2 * RET_CHUNK, 2 * RET_V_DIM), BF16),
                pltpu.VMEM((SEQ // RET_CHUNK, RET_CHUNK, 2 * RET_CHUNK), BF16),
                pltpu.VMEM((SEQ // RET_CHUNK, LANES, 2 * RET_V_DIM), F32),
                pltpu.VMEM((SEQ // RET_CHUNK, LANES, 2 * RET_V_DIM), BF16),
            ],
        ),
        out_shape=jax.ShapeDtypeStruct((b, SEQ, RET_V_W), BF16),
        compiler_params=pltpu.CompilerParams(
            dimension_semantics=("parallel", "parallel"), vmem_limit_bytes=VMEM_LIMIT),
        name="ret",
    )(log_gamma, q, k, v, gain)


def _merge_kernel(x_ref, g_ref, wg_ref, ysb_ref, yr_ref, wsb_ref, wret_ref, wo_ref, o_ref):
    x = x_ref[...]
    h = _rmsnorm(x, g_ref[...]).astype(BF16)
    g_r = _dot(h, wg_ref[:, 0:RET_V_W])
    y_r = (yr_ref[...].astype(F32) * (g_r * _sigmoid(g_r))).astype(BF16)
    gate_sb = _dot(h, wg_ref[:, RET_V_W:RET_V_W + D_MODEL])
    merged = _sigmoid(gate_sb) * _dot(ysb_ref[...], wsb_ref[...])
    gate_ret = _dot(h, wg_ref[:, RET_V_W + D_MODEL:GATE_W])
    merged = merged + _sigmoid(gate_ret) * _dot(y_r, wret_ref[...])
    o_ref[...] = x + _dot(merged.astype(BF16), wo_ref[...])


def _merge(x2d, g_mix, w_gate, y_sb, y_r, w_sb, w_ret, w_out, tm):
    t = x2d.shape[0]
    row = lambda i: (i, 0)
    const = lambda i: (0, 0)
    return pl.pallas_call(
        _merge_kernel,
        grid=(t // tm,),
        in_specs=[
            pl.BlockSpec((tm, D_MODEL), row),
            pl.BlockSpec((1, D_MODEL), const),
            pl.BlockSpec((D_MODEL, GATE_W), const),
            pl.BlockSpec((tm, SB_W), row),
            pl.BlockSpec((tm, RET_V_W), row),
            pl.BlockSpec((SB_W, D_MODEL), const),
            pl.BlockSpec((RET_V_W, D_MODEL), const),
            pl.BlockSpec((D_MODEL, D_MODEL), const),
        ],
        out_specs=pl.BlockSpec((tm, D_MODEL), row),
        out_shape=jax.ShapeDtypeStruct((t, D_MODEL), F32),
        compiler_params=pltpu.CompilerParams(
            dimension_semantics=("parallel",), vmem_limit_bytes=VMEM_LIMIT),
        name="merge",
    )(x2d, g_mix, w_gate, y_sb, y_r, w_sb, w_ret, w_out)


def _gelu_tanh(x):
    c = 0.7978845608028654
    t = jnp.tanh(x * (c + (c * 0.044715) * (x * x)))
    hx = 0.5 * x
    return hx + hx * t


def _ffn_kernel(tiles_per_seq, x_ref, p_ref, gf_ref, wup_ref, cw_ref, cb_ref, wd_ref,
                gp_ref, wpg_ref, wpp_ref, o_ref, halo_ref, h_ref, au_ref, g_ref):
    tm = x_ref.shape[0]
    h_ref[...] = _rmsnorm(x_ref[...], gf_ref[...]).astype(BF16)
    row8 = lax.broadcasted_iota(jnp.int32, (SUBLANES, FF_CHUNK), 0)

    @pl.when(pl.program_id(0) % tiles_per_seq == 0)
    def _():
        halo_ref[...] = jnp.zeros_like(halo_ref)

    def chunk_cols(ci, offset=0):
        return pl.ds(pl.multiple_of(offset + ci * FF_CHUNK, FF_CHUNK), FF_CHUNK)

    def up(ci, slot):
        h = h_ref[...]
        au_ref[slot, 0] = _dot(h, wup_ref[:, chunk_cols(ci)])
        au_ref[slot, 1] = _dot(h, wup_ref[:, chunk_cols(ci, D_FF)])

    def gated(ci, slot):
        a = au_ref[slot, 0]
        u = au_ref[slot, 1]
        prev = halo_ref[ci]
        halo_ref[ci] = a[tm - SUBLANES:tm]
        a1 = pltpu.roll(a, 1, axis=0)
        a2 = pltpu.roll(a, 2, axis=0)
        p1 = pltpu.roll(prev, 1, axis=0)
        p2 = pltpu.roll(prev, 2, axis=0)
        a1 = jnp.concatenate([jnp.where(row8 < 1, p1, a1[0:SUBLANES]), a1[SUBLANES:]], axis=0)
        a2 = jnp.concatenate([jnp.where(row8 < 2, p2, a2[0:SUBLANES]), a2[SUBLANES:]], axis=0)
        cw = cw_ref[:, chunk_cols(ci)]
        conv = cb_ref[:, chunk_cols(ci)] + cw[0:1] * a2 + cw[1:2] * a1 + cw[2:3] * a
        g_ref[:, chunk_cols(ci)] = (_gelu_tanh(conv) * u).astype(BF16)

    up(0, 0)

    def chunk_pair(k, _):
        up(2 * k + 1, 1)
        gated(2 * k, 0)
        up(2 * k + 2, 0)
        gated(2 * k + 1, 1)
        return 0

    lax.fori_loop(0, (N_FF_CHUNKS - 1) // 2, chunk_pair, 0)

    last = N_FF_CHUNKS - 1
    done = last * FF_CHUNK
    down = _dot(g_ref[:, 0:done], wd_ref[0:done, :])
    gated(last, last % 2)
    x2 = x_ref[...] + (down + _dot(g_ref[:, done:D_FF], wd_ref[done:D_FF, :]))

    emb = _dot(p_ref[...].astype(BF16), wpp_ref[...])
    h3 = _rmsnorm(x2, gp_ref[...]).astype(BF16)
    gate = _sigmoid(_dot(h3, wpg_ref[...]))
    o_ref[...] = x2 + gate * emb


def _ffn_ple(x2d, p2d, g_ffn, w_up, cw, cb, wd, g_ple, w_pg, w_pp, tm):
    t = x2d.shape[0]
    row = lambda i: (i, 0)
    c2 = lambda i: (0, 0)
    once = pl.Buffered(1)
    return pl.pallas_call(
        functools.partial(_ffn_kernel, SEQ // tm),
        grid=(t // tm,),
        in_specs=[
            pl.BlockSpec((tm, D_MODEL), row),
            pl.BlockSpec((tm, PLE_DIM), row),
            pl.BlockSpec((1, D_MODEL), c2),
            pl.BlockSpec((D_MODEL, 2 * D_FF), c2, pipeline_mode=once),
            pl.BlockSpec((CONV_WIDTH, D_FF), c2),
            pl.BlockSpec((1, D_FF), c2),
            pl.BlockSpec((D_FF, D_MODEL), c2, pipeline_mode=once),
            pl.BlockSpec((1, D_MODEL), c2),
            pl.BlockSpec((D_MODEL, D_MODEL), c2, pipeline_mode=once),
            pl.BlockSpec((PLE_DIM, D_MODEL), c2, pipeline_mode=once),
        ],
        out_specs=pl.BlockSpec((tm, D_MODEL), row),
        out_shape=jax.ShapeDtypeStruct((t, D_MODEL), F32),
        scratch_shapes=[
            pltpu.VMEM((N_FF_CHUNKS, SUBLANES, FF_CHUNK), F32),
            pltpu.VMEM((tm, D_MODEL), BF16),
            pltpu.VMEM((2, 2, tm, FF_CHUNK), F32),
            pltpu.VMEM((tm, D_FF), BF16),
        ],
        compiler_params=pltpu.CompilerParams(
            dimension_semantics=("arbitrary",), vmem_limit_bytes=VMEM_LIMIT),
        name="ffn_ple",
    )(x2d, p2d, g_ffn, w_up, cw, cb, wd, g_ple, w_pg, w_pp)


def _rotary_tables():
    half = RET_QK_DIM // 2
    inv = ROPE_BASE ** (-jnp.arange(half, dtype=F32) * 2.0 / RET_QK_DIM)
    ang = jnp.arange(SEQ, dtype=F32)[:, None] * inv[None, :]
    reps = LANES // half
    cos = jnp.tile(jnp.cos(ang), (1, reps))
    sin = jnp.tile(jnp.sin(ang), (1, reps))
    first_half = (jnp.arange(LANES) % RET_QK_DIM) < half
    return cos, jnp.where(first_half[None, :], -sin, sin)


def _layer(x, p_i, g_mix, w_in, sb_q_gain, sb_k_gain, ret_norm_gain, w_branch_sb,
           w_branch_ret, w_out, g_ffn, w_up, conv_w, conv_b, w_down, g_ple,
           w_ple_gate, w_ple_proj):
    b, s, d = x.shape
    t = b * s
    x2d = x.reshape(t, d)
    g_mix = g_mix.reshape(1, d)
    w_in = w_in.astype(BF16)
    cos_t, sin_t = _rotary_tables()
    q_gain = (sb_q_gain * (SB_HEAD_DIM ** -0.5 * LOG2E)).reshape(1, SB_W)
    k_gain = sb_k_gain.reshape(1, SB_W)

    q_sb, k_sb, v_sb, q_r, k_r, v_r = _proj(
        x2d, g_mix, w_in[:, :QKV_W], q_gain, k_gain, cos_t, sin_t, tm=512)

    y_sb = _sb_attention(q_sb.reshape(b, s, SB_W), k_sb.reshape(b, s, SB_W),
                         v_sb.reshape(b, s, SB_W))

    log_gamma = jnp.log1p(-jnp.exp2(-5.0 - jnp.arange(RET_HEADS, dtype=F32)))
    y_r = _retention(log_gamma, q_r.reshape(b, s, RET_QK_W), k_r.reshape(b, s, RET_QK_W),
                     v_r.reshape(b, s, RET_V_W), ret_norm_gain.reshape(1, RET_V_W))

    x1 = _merge(x2d, g_mix, w_in[:, QKV_W:], y_sb.reshape(t, SB_W), y_r.reshape(t, RET_V_W),
                w_branch_sb.astype(BF16), w_branch_ret.astype(BF16), w_out.astype(BF16), tm=512)

    out = _ffn_ple(
        x1, p_i.reshape(t, PLE_DIM), g_ffn.reshape(1, d),
        w_up.astype(BF16), conv_w, conv_b.reshape(1, D_FF), w_down.astype(BF16),
        g_ple.reshape(1, d), w_ple_gate.astype(BF16), w_ple_proj.astype(BF16), tm=512)
    return out.reshape(b, s, d)


def kernel(x, p, g_mix, w_in, sb_q_gain, sb_k_gain, ret_norm_gain, w_branch_sb, w_branch_ret,
           w_out, g_ffn, w_up, conv_w, conv_b, w_down, g_ple, w_ple_gate, w_ple_proj):
    for i in range(p.shape[0]):
        x = _layer(x, p[i], g_mix[i], w_in[i], sb_q_gain[i], sb_k_gain[i], ret_norm_gain[i],
                   w_branch_sb[i], w_branch_ret[i], w_out[i], g_ffn[i], w_up[i], conv_w[i],
                   conv_b[i], w_down[i], g_ple[i], w_ple_gate[i], w_ple_proj[i])
    return x
```

```python
import functools

import jax
import jax.numpy as jnp
from jax import lax
from jax.experimental import pallas as pl
from jax.experimental.pallas import tpu as pltpu

D_MODEL = 1024
BATCH = 16
SEQ = 2048
PLE_DIM = 256
SB_HEADS = 8
SB_HEAD_DIM = 64
RET_HEADS = 8
RET_QK_DIM = 64
RET_V_DIM = 128
RET_CHUNK = 128
ROPE_BASE = 10000.0
D_FF = 2816
CONV_WIDTH = 3
EPS = 1e-6

SB_W = SB_HEADS * SB_HEAD_DIM
RET_QK_W = RET_HEADS * RET_QK_DIM
RET_V_W = RET_HEADS * RET_V_DIM
QKV_W = 3 * SB_W + 2 * RET_QK_W + RET_V_W
GATE_W = RET_V_W + 2 * D_MODEL

LANES = 128
SUBLANES = 8
ATT_BLOCK = 128
ATT_QTILE = 512
ATT_UNROLL = 2
ATT_ROW_PARTS = 2
LOG2E = 1.4426950408889634
FF_CHUNK = 256
N_FF_CHUNKS = D_FF // FF_CHUNK
assert N_FF_CHUNKS * FF_CHUNK == D_FF and N_FF_CHUNKS % 2 == 1
VMEM_LIMIT = 56 * 1024 * 1024

F32 = jnp.float32
BF16 = jnp.bfloat16


def _rmsnorm(x, gain):
    ms = jnp.mean(x * x, axis=-1, keepdims=True)
    return x * lax.rsqrt(ms + EPS) * gain


def _sigmoid(x):
    return 1.0 / (1.0 + jnp.exp(-x))


def _dot(a, b):
    return jnp.dot(a, b, preferred_element_type=F32)


def _dot_nt(a, b):
    return lax.dot_general(a, b, (((1,), (1,)), ((), ())), preferred_element_type=F32)


def _dot_tn(a, b):
    return lax.dot_general(a, b, (((0,), (0,)), ((), ())), preferred_element_type=F32)


def _group_mean_matrix(width, group):
    r = lax.broadcasted_iota(jnp.int32, (width, width), 0) // group
    c = lax.broadcasted_iota(jnp.int32, (width, width), 1) // group
    return jnp.where(r == c, 1.0 / group, 0.0).astype(BF16)


def _head_rmsnorm_cols(y, gain, gmat):
    width = gmat.shape[0]
    parts = []
    for c in range(y.shape[1] // width):
        sq = y[:, c * width:(c + 1) * width]
        parts.append(_dot((sq * sq).astype(BF16), gmat))
    ms = jnp.concatenate(parts, axis=1)
    return y * lax.rsqrt(ms + EPS) * gain


def _rotary_cols(y, cos, sin_signed, first_half):
    width = y.shape[1]
    half = RET_QK_DIM // 2
    up = pltpu.roll(y, width - half, axis=1)
    dn = pltpu.roll(y, half, axis=1)
    partner = jnp.where(first_half, up, dn)
    return y * cos + partner * sin_signed


def _proj_kernel(x_ref, g_ref, w_ref, qg_ref, kg_ref, cos_ref, sin_ref,
                 qsb_ref, ksb_ref, vsb_ref, qr_ref, kr_ref, vr_ref):
    h = _rmsnorm(x_ref[...], g_ref[...]).astype(BF16)
    gmat = _group_mean_matrix(2 * LANES, SB_HEAD_DIM)

    o = 0
    q = _dot(h, w_ref[:, o:o + SB_W]); o += SB_W
    qsb_ref[...] = _head_rmsnorm_cols(q, qg_ref[...], gmat).astype(BF16)
    k = _dot(h, w_ref[:, o:o + SB_W]); o += SB_W
    ksb_ref[...] = _head_rmsnorm_cols(k, kg_ref[...], gmat).astype(BF16)
    vsb_ref[...] = _dot(h, w_ref[:, o:o + SB_W]).astype(BF16); o += SB_W

    reps = RET_QK_W // LANES
    cos = jnp.concatenate([cos_ref[...]] * reps, axis=1)
    sin = jnp.concatenate([sin_ref[...]] * reps, axis=1)
    lane = lax.broadcasted_iota(jnp.int32, (1, RET_QK_W), 1)
    first_half = (lane % RET_QK_DIM) < (RET_QK_DIM // 2)
    q = _dot(h, w_ref[:, o:o + RET_QK_W]); o += RET_QK_W
    qr_ref[...] = _rotary_cols(q, cos, sin, first_half).astype(BF16)
    k = _dot(h, w_ref[:, o:o + RET_QK_W]); o += RET_QK_W
    kr_ref[...] = (_rotary_cols(k, cos, sin, first_half) * (RET_QK_DIM ** -0.5)).astype(BF16)
    vr_ref[...] = _dot(h, w_ref[:, o:o + RET_V_W]).astype(BF16)


def _proj(x2d, g_mix, w_qkv, q_gain, k_gain, cos_t, sin_t, tm):
    t = x2d.shape[0]
    tiles_per_seq = SEQ // tm
    row = lambda i: (i, 0)
    const = lambda i: (0, 0)
    pos = lambda i: (i % tiles_per_seq, 0)
    out_w = (SB_W, SB_W, SB_W, RET_QK_W, RET_QK_W, RET_V_W)
    return pl.pallas_call(
        _proj_kernel,
        grid=(t // tm,),
        in_specs=[
            pl.BlockSpec((tm, D_MODEL), row),
            pl.BlockSpec((1, D_MODEL), const),
            pl.BlockSpec((D_MODEL, QKV_W), const),
            pl.BlockSpec((1, SB_W), const),
            pl.BlockSpec((1, SB_W), const),
            pl.BlockSpec((tm, LANES), pos),
            pl.BlockSpec((tm, LANES), pos),
        ],
        out_specs=[pl.BlockSpec((tm, w), row) for w in out_w],
        out_shape=[jax.ShapeDtypeStruct((t, w), BF16) for w in out_w],
        compiler_params=pltpu.CompilerParams(
            dimension_semantics=("parallel",), vmem_limit_bytes=VMEM_LIMIT),
        name="proj",
    )(x2d, g_mix, w_qkv, q_gain, k_gain, cos_t, sin_t)


def _sb_kernel(q_ref, k_ref, v_ref, o_ref, kst_ref, vst_ref, carry_ref, acc_ref, z_ref, a_ref):
    blk = ATT_BLOCK
    n_blocks = SEQ // blk
    qt = ATT_QTILE
    sub = qt // blk
    lane = lax.broadcasted_iota(jnp.int32, (1, LANES), 1)
    head0 = lane < SB_HEAD_DIM
    zero = jnp.zeros((), BF16)

    for j in range(n_blocks):
        kj = k_ref[0, j * blk:(j + 1) * blk, :]
        vj = v_ref[0, j * blk:(j + 1) * blk, :]
        kst_ref[j, 0:blk, :] = jnp.where(head0, kj, zero)
        kst_ref[j, blk:2 * blk, :] = jnp.where(head0, zero, kj)
        vst_ref[2 * j * blk:(2 * j + 1) * blk, :] = jnp.where(head0, vj, zero)
        vst_ref[(2 * j + 1) * blk:(2 * j + 2) * blk, :] = jnp.where(head0, zero, vj)

    r = lax.broadcasted_iota(jnp.int32, (blk, 2 * blk), 0)
    c = lax.broadcasted_iota(jnp.int32, (blk, 2 * blk), 1)
    cum_mat = jnp.where((r > c) | (c >= blk), 1.0, 0.0).astype(BF16)
    t_idx = lax.broadcasted_iota(jnp.int32, (blk, 2 * blk), 0)
    s_idx = lax.broadcasted_iota(jnp.int32, (blk, 2 * blk), 1) % blk
    causal = s_idx < t_idx

    def mask_first_block(x):
        top = jnp.where(causal, x[0:blk], 0.0)
        return top if x.shape[0] == blk else jnp.concatenate([top, x[blk:]], axis=0)

    def scores(q, j):
        return _dot_nt(q, kst_ref[j])

    def cumulate(z, diag):
        sp = jnp.maximum(z, 0.0) + jnp.log(1.0 + jnp.exp2(-jnp.abs(z))) * LOG2E
        if diag:
            sp = mask_first_block(sp)
        sp16 = sp.astype(BF16)
        s0 = _dot(sp16[:, 0:blk], cum_mat)
        s1 = _dot(sp16[:, blk:], cum_mat)
        later = jnp.concatenate([s0[:, 0:blk], s1[:, 0:blk]], axis=1)
        total = jnp.concatenate([s0[:, blk:], s1[:, blk:]], axis=1)
        return z - sp, later, total

    def weights(w, later, carry, diag):
        a = jnp.exp2(w - carry - later)
        if diag:
            a = mask_first_block(a)
        return a.astype(BF16)

    def v_rows(j_low, n):
        start = j_low * (2 * blk)
        if not isinstance(start, int):
            start = pl.multiple_of(start, 2 * blk)
        return pl.ds(start, n * 2 * blk)

    part = qt // ATT_ROW_PARTS
    row_parts = [(p * part, (p + 1) * part) for p in range(ATT_ROW_PARTS)]
    all_parts = tuple(range(ATT_ROW_PARTS))
    n_keys = ATT_UNROLL
    assert sub == 2 * n_keys and part == 2 * blk

    def issue_scores(q, j_low, slot, parts=all_parts):
        for p in parts:
            r0, r1 = row_parts[p]
            for u in range(n_keys):
                z_ref[slot, p, u] = scores(q[r0:r1], j_low + u)

    def flush_values(slot, j_low, parts=all_parts):
        for p in parts:
            r0, r1 = row_parts[p]
            acc_ref[r0:r1, :] += _dot(a_ref[slot, p], vst_ref[v_rows(j_low, n_keys), :])

    def chain_kind(p, jj):
        if jj < 2 * p:
            return "full"
        if jj == 2 * p:
            return "upper"
        if jj == 2 * p + 1:
            return "lower"
        return "skip"

    def step(q, slot, j_low, j_low_next, j_low_prev, parts=all_parts, prev_parts=all_parts,
             next_parts=all_parts, first_tile_block=None):
        if j_low_prev is not None:
            flush_values(1 - slot, j_low_prev, prev_parts)
        if j_low_next is not None:
            issue_scores(q, j_low_next, 1 - slot, next_parts)

        def kind(p, u):
            return "full" if first_tile_block is None else chain_kind(p, first_tile_block + u)

        order = list(reversed(range(n_keys)))
        cs = {}
        for p in parts:
            for u in order:
                z = z_ref[slot, p, u]
                k = kind(p, u)
                if k == "lower":
                    z = z[blk:]
                cs[p, u] = cumulate(z, k != "full")
        for p in parts:
            r0, r1 = row_parts[p]
            carry = carry_ref[r0:r1, :]
            a_parts = []
            for u in order:
                w, later, total = cs[p, u]
                k = kind(p, u)
                if k == "lower":
                    a = weights(w, later, carry[blk:], True)
                    a_parts.append(jnp.concatenate([jnp.zeros_like(a), a], axis=0))
                    carry = jnp.concatenate([carry[0:blk], carry[blk:] + total], axis=0)
                else:
                    a_parts.append(weights(w, later, carry, k == "upper"))
                    carry = carry + total
            carry_ref[r0:r1, :] = carry
            a_ref[slot, p] = jnp.concatenate(a_parts[::-1], axis=1)

    def query_tile(ti):
        rows = slice(ti * qt, (ti + 1) * qt)
        q = q_ref[0, rows, :]
        carry_ref[...] = jnp.zeros_like(carry_ref)
        acc_ref[...] = jnp.zeros_like(acc_ref)
        own = ti * sub
        upper_parts = tuple(p for p in all_parts if 2 * p + 1 >= n_keys)
        issue_scores(q, own + n_keys, 0, upper_parts)
        step(q, 0, own + n_keys, own, None, parts=upper_parts, next_parts=all_parts,
             first_tile_block=n_keys)
        step(q, 1, own, own - n_keys if ti > 0 else None, own + n_keys, prev_parts=upper_parts,
             first_tile_block=0)
        if ti > 0:
            def earlier(n, _):
                base = (ti - 1 - n) * sub
                step(q, 0, base + n_keys, base, base + 2 * n_keys)
                step(q, 1, base, jnp.maximum(base - n_keys, 0), base + n_keys)
                return 0

            lax.fori_loop(0, ti, earlier, 0)
        flush_values(1, 0)
        o_ref[0, rows, :] = acc_ref[...].astype(BF16)

    for ti in range(SEQ // qt):
        query_tile(ti)


def _sb_attention(q, k, v):
    b = q.shape[0]
    pairs = SB_W // LANES
    spec = pl.BlockSpec((1, SEQ, LANES), lambda bi, hp: (bi, 0, hp))
    stacked = pltpu.VMEM((SEQ // ATT_BLOCK, 2 * ATT_BLOCK, LANES), BF16)
    part = ATT_QTILE // ATT_ROW_PARTS
    return pl.pallas_call(
        _sb_kernel,
        grid=(b, pairs),
        in_specs=[spec, spec, spec],
        out_specs=spec,
        out_shape=jax.ShapeDtypeStruct((b, SEQ, SB_W), BF16),
        scratch_shapes=[
            stacked,
            pltpu.VMEM((2 * SEQ, LANES), BF16),
            pltpu.VMEM((ATT_QTILE, 2 * ATT_BLOCK), F32),
            pltpu.VMEM((ATT_QTILE, LANES), F32),
            pltpu.VMEM((2, ATT_ROW_PARTS, ATT_UNROLL, part, 2 * ATT_BLOCK), F32),
            pltpu.VMEM((2, ATT_ROW_PARTS, part, ATT_UNROLL * 2 * ATT_BLOCK), BF16),
        ],
        compiler_params=pltpu.CompilerParams(
            dimension_semantics=("parallel", "parallel"), vmem_limit_bytes=VMEM_LIMIT),
        name="sb_attn",
    )(q, k, v)


def _ret_kernel(lg_ref, q_ref, k_ref, v_ref, gain_ref, o_ref,
                kst_ref, vbd_ref, inner_ref, update_ref, state_ref):
    c = RET_CHUNK
    dv = RET_V_DIM
    n_chunks = SEQ // c
    hp = pl.program_id(1)
    zero = jnp.zeros((), BF16)
    qk_head0 = lax.broadcasted_iota(jnp.int32, (1, LANES), 1) < RET_QK_DIM
    col_head0 = lax.broadcasted_iota(jnp.int32, (1, 2 * dv), 1) < dv
    lg0 = lg_ref[2 * hp]
    lg1 = lg_ref[2 * hp + 1]
    lg_col = jnp.where(col_head0, lg0, lg1)
    lg_qk = jnp.where(qk_head0, lg0, lg1)

    row = lax.broadcasted_iota(jnp.int32, (c, 2 * c), 0).astype(F32)
    col = (lax.broadcasted_iota(jnp.int32, (c, 2 * c), 1) % c).astype(F32)
    diff = row - col
    d_local = jnp.where(diff >= 0.0, jnp.exp(jnp.maximum(diff, 0.0) * lg_col), 0.0)
    xi = jnp.exp((row + 1.0) * lg_col)
    g_chunk = jnp.exp(jnp.full((1, 2 * dv), c, F32) * lg_col)
    row_k = lax.broadcasted_iota(jnp.int32, (c, LANES), 0).astype(F32)
    zeta = jnp.exp((c - 1.0 - row_k) * lg_qk)
    own_block = ((lax.broadcasted_iota(jnp.int32, (LANES, 2 * dv), 0) < RET_QK_DIM)
                 == (lax.broadcasted_iota(jnp.int32, (LANES, 2 * dv), 1) < dv))

    for n in range(n_chunks):
        rows = slice(n * c, (n + 1) * c)
        kn = k_ref[0, rows, :]
        vn = v_ref[0, rows, :]
        kst_ref[n, 0:c, :] = jnp.where(qk_head0, kn, zero)
        kst_ref[n, c:2 * c, :] = jnp.where(qk_head0, zero, kn)
        vbd_ref[n, 0:c, :] = jnp.where(col_head0, vn, zero)
        vbd_ref[n, c:2 * c, :] = jnp.where(col_head0, zero, vn)

    for n in range(n_chunks):
        rows = slice(n * c, (n + 1) * c)
        inner = _dot_nt(q_ref[0, rows, :], kst_ref[n]) * d_local
        inner_ref[n] = inner.astype(BF16)
        kz = (k_ref[0, rows, :].astype(F32) * zeta).astype(BF16)
        update = _dot_tn(kz, v_ref[0, rows, :])
        update_ref[n] = jnp.where(own_block, update, 0.0)

    state = jnp.zeros((LANES, 2 * dv), F32)
    for n in range(n_chunks):
        state_ref[n] = state.astype(BF16)
        state = state * g_chunk + update_ref[n]

    for n in range(n_chunks):
        rows = slice(n * c, (n + 1) * c)
        qc = q_ref[0, rows, :]
        y = _dot(inner_ref[n], vbd_ref[n]) + _dot(qc, state_ref[n]) * xi
        for hh in range(2):
            yh = y[:, hh * dv:(hh + 1) * dv]
            ms = jnp.mean(yh * yh, axis=-1, keepdims=True)
            yh = yh * lax.rsqrt(ms + EPS) * gain_ref[:, hh * dv:(hh + 1) * dv]
            o_ref[0, rows, hh * dv:(hh + 1) * dv] = yh.astype(o_ref.dtype)


def _retention(log_gamma, q, k, v, gain):
    b = q.shape[0]
    pairs = RET_QK_W // LANES
    qk_spec = pl.BlockSpec((1, SEQ, LANES), lambda bi, hp, lg: (bi, 0, hp))
    v_spec = pl.BlockSpec((1, SEQ, 2 * RET_V_DIM), lambda bi, hp, lg: (bi, 0, hp))
    return pl.pallas_call(
        _ret_kernel,
        grid_spec=pltpu.PrefetchScalarGridSpec(
            num_scalar_prefetch=1,
            grid=(b, pairs),
            in_specs=[qk_spec, qk_spec, v_spec,
                      pl.BlockSpec((1, 2 * RET_V_DIM), lambda bi, hp, lg: (0, hp))],
            out_specs=v_spec,
            scratch_shapes=[
                pltpu.VMEM((SEQ // RET_CHUNK, 2 * RET_CHUNK, LANES), BF16),
                pltpu.VMEM((SEQ // RET_CHUNK, 2 * RET_CHUNK, 2 * RET_V_DIM), BF16),
                pltpu.VMEM((SEQ // RET_CHUNK, RET_CHUNK, 2 * RET_CHUNK), BF16),
                pltpu.VMEM((SEQ // RET_CHUNK, LANES, 2 * RET_V_DIM), F32),
                pltpu.VMEM((SEQ // RET_CHUNK, LANES, 2 * RET_V_DIM), BF16),
            ],
        ),
        out_shape=jax.ShapeDtypeStruct((b, SEQ, RET_V_W), BF16),
        compiler_params=pltpu.CompilerParams(
            dimension_semantics=("parallel", "parallel"), vmem_limit_bytes=VMEM_LIMIT),
        name="ret",
    )(log_gamma, q, k, v, gain)


def _merge_kernel(x_ref, g_ref, wg_ref, ysb_ref, yr_ref, wsb_ref, wret_ref, wo_ref, o_ref):
    x = x_ref[...]
    h = _rmsnorm(x, g_ref[...]).astype(BF16)
    g_r = _dot(h, wg_ref[:, 0:RET_V_W])
    y_r = (yr_ref[...].astype(F32) * (g_r * _sigmoid(g_r))).astype(BF16)
    gate_sb = _dot(h, wg_ref[:, RET_V_W:RET_V_W + D_MODEL])
    merged = _sigmoid(gate_sb) * _dot(ysb_ref[...], wsb_ref[...])
    gate_ret = _dot(h, wg_ref[:, RET_V_W + D_MODEL:GATE_W])
    merged = merged + _sigmoid(gate_ret) * _dot(y_r, wret_ref[...])
    o_ref[...] = x + _dot(merged.astype(BF16), wo_ref[...])


def _merge(x2d, g_mix, w_gate, y_sb, y_r, w_sb, w_ret, w_out, tm):
    t = x2d.shape[0]
    row = lambda i: (i, 0)
    const = lambda i: (0, 0)
    return pl.pallas_call(
        _merge_kernel,
        grid=(t // tm,),
        in_specs=[
            pl.BlockSpec((tm, D_MODEL), row),
            pl.BlockSpec((1, D_MODEL), const),
            pl.BlockSpec((D_MODEL, GATE_W), const),
            pl.BlockSpec((tm, SB_W), row),
            pl.BlockSpec((tm, RET_V_W), row),
            pl.BlockSpec((SB_W, D_MODEL), const),
            pl.BlockSpec((RET_V_W, D_MODEL), const),
            pl.BlockSpec((D_MODEL, D_MODEL), const),
        ],
        out_specs=pl.BlockSpec((tm, D_MODEL), row),
        out_shape=jax.ShapeDtypeStruct((t, D_MODEL), F32),
        compiler_params=pltpu.CompilerParams(
            dimension_semantics=("parallel",), vmem_limit_bytes=VMEM_LIMIT),
        name="merge",
    )(x2d, g_mix, w_gate, y_sb, y_r, w_sb, w_ret, w_out)


def _gelu_tanh(x):
    c = 0.7978845608028654
    t = jnp.tanh(x * (c + (c * 0.044715) * (x * x)))
    hx = 0.5 * x
    return hx + hx * t


def _ffn_kernel(tiles_per_seq, x_ref, p_ref, gf_ref, wup_ref, cw_ref, cb_ref, wd_ref,
                gp_ref, wpg_ref, wpp_ref, o_ref, halo_ref, h_ref, au_ref, g_ref):
    tm = x_ref.shape[0]
    h_ref[...] = _rmsnorm(x_ref[...], gf_ref[...]).astype(BF16)
    row8 = lax.broadcasted_iota(jnp.int32, (SUBLANES, FF_CHUNK), 0)

    @pl.when(pl.program_id(0) % tiles_per_seq == 0)
    def _():
        halo_ref[...] = jnp.zeros_like(halo_ref)

    def chunk_cols(ci, offset=0):
        return pl.ds(pl.multiple_of(offset + ci * FF_CHUNK, FF_CHUNK), FF_CHUNK)

    def up(ci, slot):
        h = h_ref[...]
        au_ref[slot, 0] = _dot(h, wup_ref[:, chunk_cols(ci)])
        au_ref[slot, 1] = _dot(h, wup_ref[:, chunk_cols(ci, D_FF)])

    def gated(ci, slot):
        a = au_ref[slot, 0]
        u = au_ref[slot, 1]
        prev = halo_ref[ci]
        halo_ref[ci] = a[tm - SUBLANES:tm]
        a1 = pltpu.roll(a, 1, axis=0)
        a2 = pltpu.roll(a, 2, axis=0)
        p1 = pltpu.roll(prev, 1, axis=0)
        p2 = pltpu.roll(prev, 2, axis=0)
        a1 = jnp.concatenate([jnp.where(row8 < 1, p1, a1[0:SUBLANES]), a1[SUBLANES:]], axis=0)
        a2 = jnp.concatenate([jnp.where(row8 < 2, p2, a2[0:SUBLANES]), a2[SUBLANES:]], axis=0)
        cw = cw_ref[:, chunk_cols(ci)]
        conv = cb_ref[:, chunk_cols(ci)] + cw[0:1] * a2 + cw[1:2] * a1 + cw[2:3] * a
        g_ref[:, chunk_cols(ci)] = (_gelu_tanh(conv) * u).astype(BF16)

    up(0, 0)

    def chunk_pair(k, _):
        up(2 * k + 1, 1)
        gated(2 * k, 0)
        up(2 * k + 2, 0)
        gated(2 * k + 1, 1)
        return 0

    lax.fori_loop(0, (N_FF_CHUNKS - 1) // 2, chunk_pair, 0)

    last = N_FF_CHUNKS - 1
    done = last * FF_CHUNK
    down = _dot(g_ref[:, 0:done], wd_ref[0:done, :])
    gated(last, last % 2)
    x2 = x_ref[...] + (down + _dot(g_ref[:, done:D_FF], wd_ref[done:D_FF, :]))

    emb = _dot(p_ref[...].astype(BF16), wpp_ref[...])
    h3 = _rmsnorm(x2, gp_ref[...]).astype(BF16)
    gate = _sigmoid(_dot(h3, wpg_ref[...]))
    o_ref[...] = x2 + gate * emb


def _ffn_ple(x2d, p2d, g_ffn, w_up, cw, cb, wd, g_ple, w_pg, w_pp, tm):
    t = x2d.shape[0]
    row = lambda i: (i, 0)
    c2 = lambda i: (0, 0)
    once = pl.Buffered(1)
    return pl.pallas_call(
        functools.partial(_ffn_kernel, SEQ // tm),
        grid=(t // tm,),
        in_specs=[
            pl.BlockSpec((tm, D_MODEL), row),
            pl.BlockSpec((tm, PLE_DIM), row),
            pl.BlockSpec((1, D_MODEL), c2),
            pl.BlockSpec((D_MODEL, 2 * D_FF), c2, pipeline_mode=once),
            pl.BlockSpec((CONV_WIDTH, D_FF), c2),
            pl.BlockSpec((1, D_FF), c2),
            pl.BlockSpec((D_FF, D_MODEL), c2, pipeline_mode=once),
            pl.BlockSpec((1, D_MODEL), c2),
            pl.BlockSpec((D_MODEL, D_MODEL), c2, pipeline_mode=once),
            pl.BlockSpec((PLE_DIM, D_MODEL), c2, pipeline_mode=once),
        ],
        out_specs=pl.BlockSpec((tm, D_MODEL), row),
        out_shape=jax.ShapeDtypeStruct((t, D_MODEL), F32),
        scratch_shapes=[
            pltpu.VMEM((N_FF_CHUNKS, SUBLANES, FF_CHUNK), F32),
            pltpu.VMEM((tm, D_MODEL), BF16),
            pltpu.VMEM((2, 2, tm, FF_CHUNK), F32),
            pltpu.VMEM((tm, D_FF), BF16),
        ],
        compiler_params=pltpu.CompilerParams(
            dimension_semantics=("arbitrary",), vmem_limit_bytes=VMEM_LIMIT),
        name="ffn_ple",
    )(x2d, p2d, g_ffn, w_up, cw, cb, wd, g_ple, w_pg, w_pp)


def _rotary_tables():
    half = RET_QK_DIM // 2
    inv = ROPE_BASE ** (-jnp.arange(half, dtype=F32) * 2.0 / RET_QK_DIM)
    ang = jnp.arange(SEQ, dtype=F32)[:, None] * inv[None, :]
    reps = LANES // half
    cos = jnp.tile(jnp.cos(ang), (1, reps))
    sin = jnp.tile(jnp.sin(ang), (1, reps))
    first_half = (jnp.arange(LANES) % RET_QK_DIM) < half
    return cos, jnp.where(first_half[None, :], -sin, sin)


def _layer(x, p_i, g_mix, w_in, sb_q_gain, sb_k_gain, ret_norm_gain, w_branch_sb,
           w_branch_ret, w_out, g_ffn, w_up, conv_w, conv_b, w_down, g_ple,
           w_ple_gate, w_ple_proj):
    b, s, d = x.shape
    t = b * s
    x2d = x.reshape(t, d)
    g_mix = g_mix.reshape(1, d)
    w_in = w_in.astype(BF16)
    cos_t, sin_t = _rotary_tables()
    q_gain = (sb_q_gain * (SB_HEAD_DIM ** -0.5 * LOG2E)).reshape(1, SB_W)
    k_gain = sb_k_gain.reshape(1, SB_W)

    q_sb, k_sb, v_sb, q_r, k_r, v_r = _proj(
        x2d, g_mix, w_in[:, :QKV_W], q_gain, k_gain, cos_t, sin_t, tm=512)

    y_sb = _sb_attention(q_sb.reshape(b, s, SB_W), k_sb.reshape(b, s, SB_W),
                         v_sb.reshape(b, s, SB_W))

    log_gamma = jnp.log1p(-jnp.exp2(-5.0 - jnp.arange(RET_HEADS, dtype=F32)))
    y_r = _retention(log_gamma, q_r.reshape(b, s, RET_QK_W), k_r.reshape(b, s, RET_QK_W),
                     v_r.reshape(b, s, RET_V_W), ret_norm_gain.reshape(1, RET_V_W))

    x1 = _merge(x2d, g_mix, w_in[:, QKV_W:], y_sb.reshape(t, SB_W), y_r.reshape(t, RET_V_W),
                w_branch_sb.astype(BF16), w_branch_ret.astype(BF16), w_out.astype(BF16), tm=512)

    out = _ffn_ple(
        x1, p_i.reshape(t, PLE_DIM), g_ffn.reshape(1, d),
        w_up.astype(BF16), conv_w, conv_b.reshape(1, D_FF), w_down.astype(BF16),
        g_ple.reshape(1, d), w_ple_gate.astype(BF16), w_ple_proj.astype(BF16), tm=512)
    return out.reshape(b, s, d)


def kernel(x, p, g_mix, w_in, sb_q_gain, sb_k_gain, ret_norm_gain, w_branch_sb, w_branch_ret,
           w_out, g_ffn, w_up, conv_w, conv_b, w_down, g_ple, w_ple_gate, w_ple_proj):
    for i in range(p.shape[0]):
        x = _layer(x, p[i], g_mix[i], w_in[i], sb_q_gain[i], sb_k_gain[i], ret_norm_gain[i],
                   w_branch_sb[i], w_branch_ret[i], w_out[i], g_ffn[i], w_up[i], conv_w[i],
                   conv_b[i], w_down[i], g_ple[i], w_ple_gate[i], w_ple_proj[i])
    return x
```

```python
import functools

import jax
import jax.numpy as jnp
from jax import lax
from jax.experimental import pallas as pl
from jax.experimental.pallas import tpu as pltpu

D_MODEL = 1024
BATCH = 16
SEQ = 2048
PLE_DIM = 256
SB_HEADS = 8
SB_HEAD_DIM = 64
RET_HEADS = 8
RET_QK_DIM = 64
RET_V_DIM = 128
RET_CHUNK = 128
ROPE_BASE = 10000.0
D_FF = 2816
CONV_WIDTH = 3
EPS = 1e-6

SB_W = SB_HEADS * SB_HEAD_DIM
RET_QK_W = RET_HEADS * RET_QK_DIM
RET_V_W = RET_HEADS * RET_V_DIM
QKV_W = 3 * SB_W + 2 * RET_QK_W + RET_V_W
GATE_W = RET_V_W + 2 * D_MODEL

LANES = 128
SUBLANES = 8
ATT_BLOCK = 128
ATT_QTILE = 512
ATT_UNROLL = 2
ATT_ROW_PARTS = 2
LOG2E = 1.4426950408889634
FF_CHUNK = 256
N_FF_CHUNKS = D_FF // FF_CHUNK
assert N_FF_CHUNKS * FF_CHUNK == D_FF and N_FF_CHUNKS % 2 == 1
VMEM_LIMIT = 56 * 1024 * 1024

F32 = jnp.float32
BF16 = jnp.bfloat16


def _rmsnorm(x, gain):
    ms = jnp.mean(x * x, axis=-1, keepdims=True)
    return x * lax.rsqrt(ms + EPS) * gain


def _sigmoid(x):
    return 1.0 / (1.0 + jnp.exp(-x))


def _dot(a, b):
    return jnp.dot(a, b, preferred_element_type=F32)


def _dot_nt(a, b):
    return lax.dot_general(a, b, (((1,), (1,)), ((), ())), preferred_element_type=F32)


def _dot_tn(a, b):
    return lax.dot_general(a, b, (((0,), (0,)), ((), ())), preferred_element_type=F32)


def _group_mean_matrix(width, group):
    r = lax.broadcasted_iota(jnp.int32, (width, width), 0) // group
    c = lax.broadcasted_iota(jnp.int32, (width, width), 1) // group
    return jnp.where(r == c, 1.0 / group, 0.0).astype(BF16)


def _head_rmsnorm_cols(y, gain, gmat):
    width = gmat.shape[0]
    parts = []
    for c in range(y.shape[1] // width):
        sq = y[:, c * width:(c + 1) * width]
        parts.append(_dot((sq * sq).astype(BF16), gmat))
    ms = jnp.concatenate(parts, axis=1)
    return y * lax.rsqrt(ms + EPS) * gain


def _rotary_cols(y, cos, sin_signed, first_half):
    width = y.shape[1]
    half = RET_QK_DIM // 2
    up = pltpu.roll(y, width - half, axis=1)
    dn = pltpu.roll(y, half, axis=1)
    partner = jnp.where(first_half, up, dn)
    return y * cos + partner * sin_signed


def _proj_kernel(x_ref, g_ref, w_ref, qg_ref, kg_ref, cos_ref, sin_ref,
                 qsb_ref, ksb_ref, vsb_ref, qr_ref, kr_ref, vr_ref):
    h = _rmsnorm(x_ref[...], g_ref[...]).astype(BF16)
    gmat = _group_mean_matrix(2 * LANES, SB_HEAD_DIM)

    o = 0
    q = _dot(h, w_ref[:, o:o + SB_W]); o += SB_W
    qsb_ref[...] = _head_rmsnorm_cols(q, qg_ref[...], gmat).astype(BF16)
    k = _dot(h, w_ref[:, o:o + SB_W]); o += SB_W
    ksb_ref[...] = _head_rmsnorm_cols(k, kg_ref[...], gmat).astype(BF16)
    vsb_ref[...] = _dot(h, w_ref[:, o:o + SB_W]).astype(BF16); o += SB_W

    reps = RET_QK_W // LANES
    cos = jnp.concatenate([cos_ref[...]] * reps, axis=1)
    sin = jnp.concatenate([sin_ref[...]] * reps, axis=1)
    lane = lax.broadcasted_iota(jnp.int32, (1, RET_QK_W), 1)
    first_half = (lane % RET_QK_DIM) < (RET_QK_DIM // 2)
    q = _dot(h, w_ref[:, o:o + RET_QK_W]); o += RET_QK_W
    qr_ref[...] = _rotary_cols(q, cos, sin, first_half).astype(BF16)
    k = _dot(h, w_ref[:, o:o + RET_QK_W]); o += RET_QK_W
    kr_ref[...] = (_rotary_cols(k, cos, sin, first_half) * (RET_QK_DIM ** -0.5)).astype(BF16)
    vr_ref[...] = _dot(h, w_ref[:, o:o + RET_V_W]).astype(BF16)


def _proj(x2d, g_mix, w_qkv, q_gain, k_gain, cos_t, sin_t, tm):
    t = x2d.shape[0]
    tiles_per_seq = SEQ // tm
    row = lambda i: (i, 0)
    const = lambda i: (0, 0)
    pos = lambda i: (i % tiles_per_seq, 0)
    out_w = (SB_W, SB_W, SB_W, RET_QK_W, RET_QK_W, RET_V_W)
    return pl.pallas_call(
        _proj_kernel,
        grid=(t // tm,),
        in_specs=[
            pl.BlockSpec((tm, D_MODEL), row),
            pl.BlockSpec((1, D_MODEL), const),
            pl.BlockSpec((D_MODEL, QKV_W), const),
            pl.BlockSpec((1, SB_W), const),
            pl.BlockSpec((1, SB_W), const),
            pl.BlockSpec((tm, LANES), pos),
            pl.BlockSpec((tm, LANES), pos),
        ],
        out_specs=[pl.BlockSpec((tm, w), row) for w in out_w],
        out_shape=[jax.ShapeDtypeStruct((t, w), BF16) for w in out_w],
        compiler_params=pltpu.CompilerParams(
            dimension_semantics=("parallel",), vmem_limit_bytes=VMEM_LIMIT),
        name="proj",
    )(x2d, g_mix, w_qkv, q_gain, k_gain, cos_t, sin_t)


def _sb_kernel(q_ref, k_ref, v_ref, o_ref, kst_ref, vst_ref, carry_ref, acc_ref, z_ref, a_ref):
    blk = ATT_BLOCK
    n_blocks = SEQ // blk
    qt = ATT_QTILE
    sub = qt // blk
    lane = lax.broadcasted_iota(jnp.int32, (1, LANES), 1)
    head0 = lane < SB_HEAD_DIM
    zero = jnp.zeros((), BF16)

    for j in range(n_blocks):
        kj = k_ref[0, j * blk:(j + 1) * blk, :]
        vj = v_ref[0, j * blk:(j + 1) * blk, :]
        kst_ref[j, 0:blk, :] = jnp.where(head0, kj, zero)
        kst_ref[j, blk:2 * blk, :] = jnp.where(head0, zero, kj)
        vst_ref[2 * j * blk:(2 * j + 1) * blk, :] = jnp.where(head0, vj, zero)
        vst_ref[(2 * j + 1) * blk:(2 * j + 2) * blk, :] = jnp.where(head0, zero, vj)

    r = lax.broadcasted_iota(jnp.int32, (blk, 2 * blk), 0)
    c = lax.broadcasted_iota(jnp.int32, (blk, 2 * blk), 1)
    cum_mat = jnp.where((r > c) | (c >= blk), 1.0, 0.0).astype(BF16)
    t_idx = lax.broadcasted_iota(jnp.int32, (blk, 2 * blk), 0)
    s_idx = lax.broadcasted_iota(jnp.int32, (blk, 2 * blk), 1) % blk
    causal = s_idx < t_idx

    def mask_first_block(x):
        top = jnp.where(causal, x[0:blk], 0.0)
        return top if x.shape[0] == blk else jnp.concatenate([top, x[blk:]], axis=0)

    def scores(q, j):
        return _dot_nt(q, kst_ref[j])

    def cumulate(z, diag):
        sp = jnp.maximum(z, 0.0) + jnp.log(1.0 + jnp.exp2(-jnp.abs(z))) * LOG2E
        if diag:
            sp = mask_first_block(sp)
        sp16 = sp.astype(BF16)
        s0 = _dot(sp16[:, 0:blk], cum_mat)
        s1 = _dot(sp16[:, blk:], cum_mat)
        later = jnp.concatenate([s0[:, 0:blk], s1[:, 0:blk]], axis=1)
        total = jnp.concatenate([s0[:, blk:], s1[:, blk:]], axis=1)
        return z - sp, later, total

    def weights(w, later, carry, diag):
        a = jnp.exp2(w - carry - later)
        if diag:
            a = mask_first_block(a)
        return a.astype(BF16)

    def v_rows(j_low, n):
        start = j_low * (2 * blk)
        if not isinstance(start, int):
            start = pl.multiple_of(start, 2 * blk)
        return pl.ds(start, n * 2 * blk)

    part = qt // ATT_ROW_PARTS
    row_parts = [(p * part, (p + 1) * part) for p in range(ATT_ROW_PARTS)]
    all_parts = tuple(range(ATT_ROW_PARTS))
    n_keys = ATT_UNROLL
    assert sub == 2 * n_keys and part == 2 * blk

    def issue_scores(q, j_low, slot, parts=all_parts):
        for p in parts:
            r0, r1 = row_parts[p]
            for u in range(n_keys):
                z_ref[slot, p, u] = scores(q[r0:r1], j_low + u)

    def flush_values(slot, j_low, parts=all_parts):
        for p in parts:
            r0, r1 = row_parts[p]
            acc_ref[r0:r1, :] += _dot(a_ref[slot, p], vst_ref[v_rows(j_low, n_keys), :])

    def chain_kind(p, jj):
        if jj < 2 * p:
            return "full"
        if jj == 2 * p:
            return "upper"
        if jj == 2 * p + 1:
            return "lower"
        return "skip"

    def step(q, slot, j_low, j_low_next, j_low_prev, parts=all_parts, prev_parts=all_parts,
             next_parts=all_parts, first_tile_block=None):
        if j_low_prev is not None:
            flush_values(1 - slot, j_low_prev, prev_parts)
        if j_low_next is not None:
            issue_scores(q, j_low_next, 1 - slot, next_parts)

        def kind(p, u):
            return "full" if first_tile_block is None else chain_kind(p, first_tile_block + u)

        order = list(reversed(range(n_keys)))
        cs = {}
        for p in parts:
            for u in order:
                z = z_ref[slot, p, u]
                k = kind(p, u)
                if k == "lower":
                    z = z[blk:]
                cs[p, u] = cumulate(z, k != "full")
        for p in parts:
            r0, r1 = row_parts[p]
            carry = carry_ref[r0:r1, :]
            a_parts = []
            for u in order:
                w, later, total = cs[p, u]
                k = kind(p, u)
                if k == "lower":
                    a = weights(w, later, carry[blk:], True)
                    a_parts.append(jnp.concatenate([jnp.zeros_like(a), a], axis=0))
                    carry = jnp.concatenate([carry[0:blk], carry[blk:] + total], axis=0)
                else:
                    a_parts.append(weights(w, later, carry, k == "upper"))
                    carry = carry + total
            carry_ref[r0:r1, :] = carry
            a_ref[slot, p] = jnp.concatenate(a_parts[::-1], axis=1)

    def query_tile(ti):
        rows = slice(ti * qt, (ti + 1) * qt)
        q = q_ref[0, rows, :]
        carry_ref[...] = jnp.zeros_like(carry_ref)
        acc_ref[...] = jnp.zeros_like(acc_ref)
        own = ti * sub
        upper_parts = tuple(p for p in all_parts if 2 * p + 1 >= n_keys)
        issue_scores(q, own + n_keys, 0, upper_parts)
        step(q, 0, own + n_keys, own, None, parts=upper_parts, next_parts=all_parts,
             first_tile_block=n_keys)
        step(q, 1, own, own - n_keys if ti > 0 else None, own + n_keys, prev_parts=upper_parts,
             first_tile_block=0)
        if ti > 0:
            def earlier(n, _):
                base = (ti - 1 - n) * sub
                step(q, 0, base + n_keys, base, base + 2 * n_keys)
                step(q, 1, base, jnp.maximum(base - n_keys, 0), base + n_keys)
                return 0

            lax.fori_loop(0, ti, earlier, 0)
        flush_values(1, 0)
        o_ref[0, rows, :] = acc_ref[...].astype(BF16)

    for ti in range(SEQ // qt):
        query_tile(ti)


def _ret_kernel(lg_ref, q_ref, k_ref, v_ref, gain_ref, o_ref,
                kst_ref, vbd_ref, inner_ref, update_ref, state_ref):
    c = RET_CHUNK
    dv = RET_V_DIM
    n_chunks = SEQ // c
    hp = pl.program_id(1)
    zero = jnp.zeros((), BF16)
    qk_head0 = lax.broadcasted_iota(jnp.int32, (1, LANES), 1) < RET_QK_DIM
    col_head0 = lax.broadcasted_iota(jnp.int32, (1, 2 * dv), 1) < dv
    lg0 = lg_ref[2 * hp]
    lg1 = lg_ref[2 * hp + 1]
    lg_col = jnp.where(col_head0, lg0, lg1)
    lg_qk = jnp.where(qk_head0, lg0, lg1)

    row = lax.broadcasted_iota(jnp.int32, (c, 2 * c), 0).astype(F32)
    col = (lax.broadcasted_iota(jnp.int32, (c, 2 * c), 1) % c).astype(F32)
    diff = row - col
    d_local = jnp.where(diff >= 0.0, jnp.exp(jnp.maximum(diff, 0.0) * lg_col), 0.0)
    xi = jnp.exp((row + 1.0) * lg_col)
    g_chunk = jnp.exp(jnp.full((1, 2 * dv), c, F32) * lg_col)
    row_k = lax.broadcasted_iota(jnp.int32, (c, LANES), 0).astype(F32)
    zeta = jnp.exp((c - 1.0 - row_k) * lg_qk)
    own_block = ((lax.broadcasted_iota(jnp.int32, (LANES, 2 * dv), 0) < RET_QK_DIM)
                 == (lax.broadcasted_iota(jnp.int32, (LANES, 2 * dv), 1) < dv))

    for n in range(n_chunks):
        rows = slice(n * c, (n + 1) * c)
        kn = k_ref[0, rows, :]
        vn = v_ref[0, rows, :]
        kst_ref[n, 0:c, :] = jnp.where(qk_head0, kn, zero)
        kst_ref[n, c:2 * c, :] = jnp.where(qk_head0, zero, kn)
        vbd_ref[n, 0:c, :] = jnp.where(col_head0, vn, zero)
        vbd_ref[n, c:2 * c, :] = jnp.where(col_head0, zero, vn)

    for n in range(n_chunks):
        rows = slice(n * c, (n + 1) * c)
        inner = _dot_nt(q_ref[0, rows, :], kst_ref[n]) * d_local
        inner_ref[n] = inner.astype(BF16)
        kz = (k_ref[0, rows, :].astype(F32) * zeta).astype(BF16)
        update = _dot_tn(kz, v_ref[0, rows, :])
        update_ref[n] = jnp.where(own_block, update, 0.0)

    state = jnp.zeros((LANES, 2 * dv), F32)
    for n in range(n_chunks):
        state_ref[n] = state.astype(BF16)
        state = state * g_chunk + update_ref[n]

    for n in range(n_chunks):
        rows = slice(n * c, (n + 1) * c)
        qc = q_ref[0, rows, :]
        y = _dot(inner_ref[n], vbd_ref[n]) + _dot(qc, state_ref[n]) * xi
        for hh in range(2):
            yh = y[:, hh * dv:(hh + 1) * dv]
            ms = jnp.mean(yh * yh, axis=-1, keepdims=True)
            yh = yh * lax.rsqrt(ms + EPS) * gain_ref[:, hh * dv:(hh + 1) * dv]
            o_ref[0, rows, hh * dv:(hh + 1) * dv] = yh.astype(o_ref.dtype)


N_SB_SCRATCH = 6


def _mixers_kernel(lg_ref, q_ref, k_ref, v_ref, qr_ref, kr_ref, vr_ref, gain_ref,
                   osb_ref, oret_ref, *scratch):
    _ret_kernel(lg_ref, qr_ref, kr_ref, vr_ref, gain_ref, oret_ref, *scratch[N_SB_SCRATCH:])
    _sb_kernel(q_ref, k_ref, v_ref, osb_ref, *scratch[:N_SB_SCRATCH])


def _mixers(log_gamma, q, k, v, q_r, k_r, v_r, gain):
    b = q.shape[0]
    pairs = SB_W // LANES
    assert pairs == RET_QK_W // LANES
    part = ATT_QTILE // ATT_ROW_PARTS
    n_chunks = SEQ // RET_CHUNK
    lane_spec = pl.BlockSpec((1, SEQ, LANES), lambda bi, hp, lg: (bi, 0, hp))
    v_spec = pl.BlockSpec((1, SEQ, 2 * RET_V_DIM), lambda bi, hp, lg: (bi, 0, hp))
    stacked = pltpu.VMEM((SEQ // ATT_BLOCK, 2 * ATT_BLOCK, LANES), BF16)
    return pl.pallas_call(
        _mixers_kernel,
        grid_spec=pltpu.PrefetchScalarGridSpec(
            num_scalar_prefetch=1,
            grid=(b, pairs),
            in_specs=[lane_spec, lane_spec, lane_spec, lane_spec, lane_spec, v_spec,
                      pl.BlockSpec((1, 2 * RET_V_DIM), lambda bi, hp, lg: (0, hp))],
            out_specs=[lane_spec, v_spec],
            scratch_shapes=[
                stacked,
                pltpu.VMEM((2 * SEQ, LANES), BF16),
                pltpu.VMEM((ATT_QTILE, 2 * ATT_BLOCK), F32),
                pltpu.VMEM((ATT_QTILE, LANES), F32),
                pltpu.VMEM((2, ATT_ROW_PARTS, ATT_UNROLL, part, 2 * ATT_BLOCK), F32),
                pltpu.VMEM((2, ATT_ROW_PARTS, part, ATT_UNROLL * 2 * ATT_BLOCK), BF16),
                pltpu.VMEM((n_chunks, 2 * RET_CHUNK, LANES), BF16),
                pltpu.VMEM((n_chunks, 2 * RET_CHUNK, 2 * RET_V_DIM), BF16),
                pltpu.VMEM((n_chunks, RET_CHUNK, 2 * RET_CHUNK), BF16),
                pltpu.VMEM((n_chunks, LANES, 2 * RET_V_DIM), F32),
                pltpu.VMEM((n_chunks, LANES, 2 * RET_V_DIM), BF16),
            ],
        ),
        out_shape=[jax.ShapeDtypeStruct((b, SEQ, SB_W), BF16),
                   jax.ShapeDtypeStruct((b, SEQ, RET_V_W), BF16)],
        compiler_params=pltpu.CompilerParams(
            dimension_semantics=("parallel", "parallel"), vmem_limit_bytes=VMEM_LIMIT),
        name="mixers",
    )(log_gamma, q, k, v, q_r, k_r, v_r, gain)


def _merge_kernel(x_ref, g_ref, wg_ref, ysb_ref, yr_ref, wsb_ref, wret_ref, wo_ref, o_ref):
    x = x_ref[...]
    h = _rmsnorm(x, g_ref[...]).astype(BF16)
    g_r = _dot(h, wg_ref[:, 0:RET_V_W])
    y_r = (yr_ref[...].astype(F32) * (g_r * _sigmoid(g_r))).astype(BF16)
    gate_sb = _dot(h, wg_ref[:, RET_V_W:RET_V_W + D_MODEL])
    merged = _sigmoid(gate_sb) * _dot(ysb_ref[...], wsb_ref[...])
    gate_ret = _dot(h, wg_ref[:, RET_V_W + D_MODEL:GATE_W])
    merged = merged + _sigmoid(gate_ret) * _dot(y_r, wret_ref[...])
    o_ref[...] = x + _dot(merged.astype(BF16), wo_ref[...])


def _merge(x2d, g_mix, w_gate, y_sb, y_r, w_sb, w_ret, w_out, tm):
    t = x2d.shape[0]
    row = lambda i: (i, 0)
    const = lambda i: (0, 0)
    return pl.pallas_call(
        _merge_kernel,
        grid=(t // tm,),
        in_specs=[
            pl.BlockSpec((tm, D_MODEL), row),
            pl.BlockSpec((1, D_MODEL), const),
            pl.BlockSpec((D_MODEL, GATE_W), const),
            pl.BlockSpec((tm, SB_W), row),
            pl.BlockSpec((tm, RET_V_W), row),
            pl.BlockSpec((SB_W, D_MODEL), const),
            pl.BlockSpec((RET_V_W, D_MODEL), const),
            pl.BlockSpec((D_MODEL, D_MODEL), const),
        ],
        out_specs=pl.BlockSpec((tm, D_MODEL), row),
        out_shape=jax.ShapeDtypeStruct((t, D_MODEL), F32),
        compiler_params=pltpu.CompilerParams(
            dimension_semantics=("parallel",), vmem_limit_bytes=VMEM_LIMIT),
        name="merge",
    )(x2d, g_mix, w_gate, y_sb, y_r, w_sb, w_ret, w_out)


def _gelu_tanh(x):
    c = 0.7978845608028654
    t = jnp.tanh(x * (c + (c * 0.044715) * (x * x)))
    hx = 0.5 * x
    return hx + hx * t


def _ffn_kernel(tiles_per_seq, x_ref, p_ref, gf_ref, wup_ref, cw_ref, cb_ref, wd_ref,
                gp_ref, wpg_ref, wpp_ref, o_ref, halo_ref, h_ref, au_ref, g_ref):
    tm = x_ref.shape[0]
    h_ref[...] = _rmsnorm(x_ref[...], gf_ref[...]).astype(BF16)
    row8 = lax.broadcasted_iota(jnp.int32, (SUBLANES, FF_CHUNK), 0)

    @pl.when(pl.program_id(0) % tiles_per_seq == 0)
    def _():
        halo_ref[...] = jnp.zeros_like(halo_ref)

    def chunk_cols(ci, offset=0):
        return pl.ds(pl.multiple_of(offset + ci * FF_CHUNK, FF_CHUNK), FF_CHUNK)

    def up(ci, slot):
        h = h_ref[...]
        au_ref[slot, 0] = _dot(h, wup_ref[:, chunk_cols(ci)])
        au_ref[slot, 1] = _dot(h, wup_ref[:, chunk_cols(ci, D_FF)])

    def gated(ci, slot):
        a = au_ref[slot, 0]
        u = au_ref[slot, 1]
        prev = halo_ref[ci]
        halo_ref[ci] = a[tm - SUBLANES:tm]
        a1 = pltpu.roll(a, 1, axis=0)
        a2 = pltpu.roll(a, 2, axis=0)
        p1 = pltpu.roll(prev, 1, axis=0)
        p2 = pltpu.roll(prev, 2, axis=0)
        a1 = jnp.concatenate([jnp.where(row8 < 1, p1, a1[0:SUBLANES]), a1[SUBLANES:]], axis=0)
        a2 = jnp.concatenate([jnp.where(row8 < 2, p2, a2[0:SUBLANES]), a2[SUBLANES:]], axis=0)
        cw = cw_ref[:, chunk_cols(ci)]
        conv = cb_ref[:, chunk_cols(ci)] + cw[0:1] * a2 + cw[1:2] * a1 + cw[2:3] * a
        g_ref[:, chunk_cols(ci)] = (_gelu_tanh(conv) * u).astype(BF16)

    up(0, 0)

    def chunk_pair(k, _):
        up(2 * k + 1, 1)
        gated(2 * k, 0)
        up(2 * k + 2, 0)
        gated(2 * k + 1, 1)
        return 0

    lax.fori_loop(0, (N_FF_CHUNKS - 1) // 2, chunk_pair, 0)

    last = N_FF_CHUNKS - 1
    done = last * FF_CHUNK
    down = _dot(g_ref[:, 0:done], wd_ref[0:done, :])
    gated(last, last % 2)
    x2 = x_ref[...] + (down + _dot(g_ref[:, done:D_FF], wd_ref[done:D_FF, :]))

    emb = _dot(p_ref[...].astype(BF16), wpp_ref[...])
    h3 = _rmsnorm(x2, gp_ref[...]).astype(BF16)
    gate = _sigmoid(_dot(h3, wpg_ref[...]))
    o_ref[...] = x2 + gate * emb


def _ffn_ple(x2d, p2d, g_ffn, w_up, cw, cb, wd, g_ple, w_pg, w_pp, tm):
    t = x2d.shape[0]
    row = lambda i: (i, 0)
    c2 = lambda i: (0, 0)
    once = pl.Buffered(1)
    return pl.pallas_call(
        functools.partial(_ffn_kernel, SEQ // tm),
        grid=(t // tm,),
        in_specs=[
            pl.BlockSpec((tm, D_MODEL), row),
            pl.BlockSpec((tm, PLE_DIM), row),
            pl.BlockSpec((1, D_MODEL), c2),
            pl.BlockSpec((D_MODEL, 2 * D_FF), c2, pipeline_mode=once),
            pl.BlockSpec((CONV_WIDTH, D_FF), c2),
            pl.BlockSpec((1, D_FF), c2),
            pl.BlockSpec((D_FF, D_MODEL), c2, pipeline_mode=once),
            pl.BlockSpec((1, D_MODEL), c2),
            pl.BlockSpec((D_MODEL, D_MODEL), c2, pipeline_mode=once),
            pl.BlockSpec((PLE_DIM, D_MODEL), c2, pipeline_mode=once),
        ],
        out_specs=pl.BlockSpec((tm, D_MODEL), row),
        out_shape=jax.ShapeDtypeStruct((t, D_MODEL), F32),
        scratch_shapes=[
            pltpu.VMEM((N_FF_CHUNKS, SUBLANES, FF_CHUNK), F32),
            pltpu.VMEM((tm, D_MODEL), BF16),
            pltpu.VMEM((2, 2, tm, FF_CHUNK), F32),
            pltpu.VMEM((tm, D_FF), BF16),
        ],
        compiler_params=pltpu.CompilerParams(
            dimension_semantics=("arbitrary",), vmem_limit_bytes=VMEM_LIMIT),
        name="ffn_ple",
    )(x2d, p2d, g_ffn, w_up, cw, cb, wd, g_ple, w_pg, w_pp)


def _rotary_tables():
    half = RET_QK_DIM // 2
    inv = ROPE_BASE ** (-jnp.arange(half, dtype=F32) * 2.0 / RET_QK_DIM)
    ang = jnp.arange(SEQ, dtype=F32)[:, None] * inv[None, :]
    reps = LANES // half
    cos = jnp.tile(jnp.cos(ang), (1, reps))
    sin = jnp.tile(jnp.sin(ang), (1, reps))
    first_half = (jnp.arange(LANES) % RET_QK_DIM) < half
    return cos, jnp.where(first_half[None, :], -sin, sin)


def _layer(x, p_i, g_mix, w_in, sb_q_gain, sb_k_gain, ret_norm_gain, w_branch_sb,
           w_branch_ret, w_out, g_ffn, w_up, conv_w, conv_b, w_down, g_ple,
           w_ple_gate, w_ple_proj):
    b, s, d = x.shape
    t = b * s
    x2d = x.reshape(t, d)
    g_mix = g_mix.reshape(1, d)
    w_qkv = w_in[:, :QKV_W].astype(BF16)
    w_gate = w_in[:, QKV_W:].astype(BF16)
    cos_t, sin_t = _rotary_tables()
    q_gain = (sb_q_gain * (SB_HEAD_DIM ** -0.5 * LOG2E)).reshape(1, SB_W)
    k_gain = sb_k_gain.reshape(1, SB_W)

    q_sb, k_sb, v_sb, q_r, k_r, v_r = _proj(
        x2d, g_mix, w_qkv, q_gain, k_gain, cos_t, sin_t, tm=512)

    log_gamma = jnp.log1p(-jnp.exp2(-5.0 - jnp.arange(RET_HEADS, dtype=F32)))
    y_sb, y_r = _mixers(
        log_gamma, q_sb.reshape(b, s, SB_W), k_sb.reshape(b, s, SB_W), v_sb.reshape(b, s, SB_W),
        q_r.reshape(b, s, RET_QK_W), k_r.reshape(b, s, RET_QK_W), v_r.reshape(b, s, RET_V_W),
        ret_norm_gain.reshape(1, RET_V_W))

    x1 = _merge(x2d, g_mix, w_gate, y_sb.reshape(t, SB_W), y_r.reshape(t, RET_V_W),
                w_branch_sb.astype(BF16), w_branch_ret.astype(BF16), w_out.astype(BF16), tm=512)

    out = _ffn_ple(
        x1, p_i.reshape(t, PLE_DIM), g_ffn.reshape(1, d),
        w_up.astype(BF16), conv_w, conv_b.reshape(1, D_FF), w_down.astype(BF16),
        g_ple.reshape(1, d), w_ple_gate.astype(BF16), w_ple_proj.astype(BF16), tm=512)
    return out.reshape(b, s, d)


def kernel(x, p, g_mix, w_in, sb_q_gain, sb_k_gain, ret_norm_gain, w_branch_sb, w_branch_ret,
           w_out, g_ffn, w_up, conv_w, conv_b, w_down, g_ple, w_ple_gate, w_ple_proj):
    for i in range(p.shape[0]):
        x = _layer(x, p[i], g_mix[i], w_in[i], sb_q_gain[i], sb_k_gain[i], ret_norm_gain[i],
                   w_branch_sb[i], w_branch_ret[i], w_out[i], g_ffn[i], w_up[i], conv_w[i],
                   conv_b[i], w_down[i], g_ple[i], w_ple_gate[i], w_ple_proj[i])
    return x
```
